```python
import jax, jax.numpy as jnp
from jax import lax
import numpy as np

D_MODEL = 2048
BATCH = 4
SEQ = 4096
DEPTH = 2

MLSTM_HEADS = 4
MLSTM_QK_DIM = 256
MLSTM_V_DIM = 512
MLSTM_QK_WIDTH = MLSTM_HEADS * MLSTM_QK_DIM
MLSTM_WIDTH = MLSTM_HEADS * MLSTM_V_DIM
CHUNK = 64
N_GATE_COLS = 4 * MLSTM_HEADS
LRU_WIDTH = D_MODEL
LRU_BLOCKS = 16
LRU_BLOCK_DIM = LRU_WIDTH // LRU_BLOCKS
LRU_C = 8.0
CONV_W = 4
CONV_LEFT = 2
N_DIR = 2
NORM_EPS = 1e-6
N_IN = 2 * MLSTM_QK_WIDTH + 3 * MLSTM_WIDTH + N_GATE_COLS + 2 * LRU_WIDTH + 2 * D_MODEL

kernel_name = "hybrid_mlstm_rglru_gated_parallel_encoder"


def _rms_norm(x, g):
    xf = x.astype(jnp.float32)
    y = xf * lax.rsqrt(jnp.mean(xf * xf, axis=-1, keepdims=True) + NORM_EPS)
    return (y * g.astype(jnp.float32)).astype(x.dtype)


def _to_heads(t, head_dim):
    b, s, _ = t.shape
    return t.reshape(b, s, -1, head_dim).transpose(0, 2, 1, 3).astype(jnp.float32)


def _mlstm_one_direction(q, k, v, i_pre, f_pre):
    bsz, nh, s, _ = q.shape
    nc = s // CHUNK

    def chunks(t):
        return jnp.moveaxis(t.reshape((bsz, nh, nc, CHUNK) + t.shape[3:]), 2, 0)

    b_cum = jnp.cumsum(chunks(jax.nn.log_sigmoid(f_pre)), axis=-1)
    lower = jnp.tril(jnp.ones((CHUNK, CHUNK), dtype=bool))

    def step(carry, xs):
        c_st, n_st, m_st = carry
        qc, kc, vc, ic, bc = xs
        d_log = jnp.where(lower, bc[..., :, None] - bc[..., None, :] + ic[..., None, :], -jnp.inf)
        inter_log = bc + m_st[..., None]
        m_row = jnp.maximum(inter_log, jnp.max(d_log, axis=-1))
        w_intra = jnp.exp(d_log - m_row[..., None])
        w_inter = jnp.exp(inter_log - m_row)
        scores = jnp.einsum('bhjk,bhsk->bhjs', qc, kc) * w_intra
        num = (jnp.einsum('bhjs,bhsv->bhjv', scores, vc)
               + w_inter[..., None] * jnp.einsum('bhjk,bhvk->bhjv', qc, c_st))
        den = jnp.sum(scores, axis=-1) + w_inter * jnp.einsum('bhjk,bhk->bhj', qc, n_st)
        h = num / jnp.maximum(jnp.abs(den), jnp.exp(-m_row))[..., None]
        g_tot = bc[..., -1]
        w_log = g_tot[..., None] - bc + ic
        m_new = jnp.maximum(g_tot + m_st, jnp.max(w_log, axis=-1))
        w_k = jnp.exp(w_log - m_new[..., None])
        decay = jnp.exp(g_tot + m_st - m_new)
        c_new = decay[..., None, None] * c_st + jnp.einsum('bhsv,bhsk->bhvk', vc * w_k[..., None], kc)
        n_new = decay[..., None] * n_st + jnp.einsum('bhs,bhsk->bhk', w_k, kc)
        return (c_new, n_new, m_new), h

    init = (jnp.zeros((bsz, nh, MLSTM_V_DIM, MLSTM_QK_DIM), jnp.float32),
            jnp.zeros((bsz, nh, MLSTM_QK_DIM), jnp.float32),
            jnp.zeros((bsz, nh), jnp.float32))
    _, h = lax.scan(step, init, (chunks(q), chunks(k), chunks(v), chunks(i_pre), b_cum))
    return jnp.moveaxis(h, 0, 2).reshape(bsz, nh, s, MLSTM_V_DIM)


def _mlstm_branch(q_p, k_p, v_p, o_p, z_p, gif_p, b_if, head_g):
    bsz, s, _ = v_p.shape
    q = _to_heads(q_p, MLSTM_QK_DIM)
    k = _to_heads(k_p, MLSTM_QK_DIM) * (MLSTM_QK_DIM ** -0.5)
    v = _to_heads(v_p, MLSTM_V_DIM)
    gates = (gif_p.astype(jnp.float32) + b_if.astype(jnp.float32)).transpose(0, 2, 1)
    i_f, i_b, f_f, f_b = jnp.split(gates, 4, axis=1)
    flip = lambda t: jnp.flip(t, axis=2)
    h_fwd = _mlstm_one_direction(q, k, v, i_f, f_f)
    h_bwd = flip(_mlstm_one_direction(flip(q), flip(k), flip(v), flip(i_b), flip(f_b)))
    h = h_fwd + h_bwd
    h = h * lax.rsqrt(jnp.mean(h * h, axis=-1, keepdims=True) + NORM_EPS)
    h = h.transpose(0, 2, 1, 3).reshape(bsz, s, MLSTM_WIDTH) * head_g.astype(jnp.float32)
    h = h * jax.nn.sigmoid(o_p.astype(jnp.float32)) * jax.nn.silu(z_p.astype(jnp.float32))
    return h.astype(v_p.dtype)


def _rglru_one_direction(xc, w_r, b_r, w_i, b_i, lam, reverse):
    bsz, s, w = xc.shape
    xb = xc.reshape(bsz, s, LRU_BLOCKS, LRU_BLOCK_DIM)
    r = jax.nn.sigmoid(jnp.einsum('bsnc,ncd->bsnd', xb, w_r.astype(jnp.float32)).reshape(bsz, s, w)
                       + b_r.astype(jnp.float32))
    i = jax.nn.sigmoid(jnp.einsum('bsnc,ncd->bsnd', xb, w_i.astype(jnp.float32)).reshape(bsz, s, w)
                       + b_i.astype(jnp.float32))
    log_a = -LRU_C * r * jax.nn.softplus(-lam.astype(jnp.float32))
    a = jnp.exp(log_a)
    u = jnp.sqrt(-jnp.expm1(2.0 * log_a)) * (i * xc)

    def combine(left, right):
        a1, b1 = left
        a2, b2 = right
        return a1 * a2, a2 * b1 + b2

    _, h = lax.associative_scan(combine, (a, u), reverse=reverse, axis=1)
    return h


def _rglru_branch(x_p, z_p, conv_w, conv_b, w_rg, b_rg, lam):
    s = x_p.shape[1]
    xf = x_p.astype(jnp.float32)
    xpad = jnp.pad(xf, ((0, 0), (CONV_LEFT, CONV_W - 1 - CONV_LEFT), (0, 0)))
    cw = conv_w.astype(jnp.float32)
    xc = conv_b.astype(jnp.float32) + sum(xpad[:, t:t + s] * cw[t] for t in range(CONV_W))
    h = (_rglru_one_direction(xc, w_rg[0, 0], b_rg[0, 0], w_rg[0, 1], b_rg[0, 1], lam[0], False)
         + _rglru_one_direction(xc, w_rg[1, 0], b_rg[1, 0], w_rg[1, 1], b_rg[1, 1], lam[1], True))
    return (h * jax.nn.silu(z_p.astype(jnp.float32))).astype(x_p.dtype)


def _hybrid_layer(x, norm_g, w_in, b_if, head_g, conv_w, conv_b, w_rg, b_rg, lam,
                  w_branch_a, w_branch_b, w_out):
    h = _rms_norm(x, norm_g)
    proj = jnp.einsum('bsd,dn->bsn', h, w_in.astype(h.dtype))
    sizes = [MLSTM_QK_WIDTH, MLSTM_QK_WIDTH, MLSTM_WIDTH, MLSTM_WIDTH, MLSTM_WIDTH,
             N_GATE_COLS, LRU_WIDTH, LRU_WIDTH, D_MODEL]
    cuts = [int(c) for c in np.cumsum(sizes)]
    q_p, k_p, v_p, o_p, za_p, gif_p, xb_p, zb_p, ga_p, gb_p = jnp.split(proj, cuts, axis=-1)
    ya = _mlstm_branch(q_p, k_p, v_p, o_p, za_p, gif_p, b_if, head_g)
    yb = _rglru_branch(xb_p, zb_p, conv_w, conv_b, w_rg, b_rg, lam)
    ya = jnp.einsum('bsw,wd->bsd', ya, w_branch_a.astype(ya.dtype))
    yb = jnp.einsum('bsw,wd->bsd', yb, w_branch_b.astype(yb.dtype))
    merged = (jax.nn.sigmoid(ga_p.astype(jnp.float32)) * ya.astype(jnp.float32)
              + jax.nn.sigmoid(gb_p.astype(jnp.float32)) * yb.astype(jnp.float32)).astype(x.dtype)
    return x + jnp.einsum('bsd,de->bse', merged, w_out.astype(x.dtype))


def setup_inputs(seed: int = 0) -> dict:
    key = jax.random.key(seed)
    ks = jax.random.split(key, 16)
    f32 = jnp.float32
    nrm = lambda k, shape, scale: jax.random.normal(k, shape, f32) * scale
    x = jax.random.normal(ks[0], (BATCH, SEQ, D_MODEL), f32)
    norm_g = 1.0 + nrm(ks[1], (DEPTH, D_MODEL), 0.02)
    w_in = nrm(ks[2], (DEPTH, D_MODEL, N_IN), D_MODEL ** -0.5)
    i_bias = nrm(ks[3], (DEPTH, 2 * MLSTM_HEADS), 0.1)
    f_bias = 3.0 + 3.0 * jax.random.uniform(ks[4], (DEPTH, 2 * MLSTM_HEADS), f32)
    b_if = jnp.concatenate([i_bias, f_bias], axis=-1)
    head_g = 1.0 + nrm(ks[5], (DEPTH, MLSTM_WIDTH), 0.02)
    conv_w = nrm(ks[6], (DEPTH, CONV_W, LRU_WIDTH), CONV_W ** -0.5)
    conv_b = nrm(ks[7], (DEPTH, LRU_WIDTH), 0.02)
    w_rg = nrm(ks[8], (DEPTH, N_DIR, 2, LRU_BLOCKS, LRU_BLOCK_DIM, LRU_BLOCK_DIM), LRU_BLOCK_DIM ** -0.5)
    b_rg = nrm(ks[9], (DEPTH, N_DIR, 2, LRU_WIDTH), 0.02)
    u = jax.random.uniform(ks[10], (DEPTH, N_DIR, LRU_WIDTH), f32, 0.9, 0.999)
    lru_lambda = jnp.log(u) - jnp.log1p(-u)
    w_branch_a = nrm(ks[11], (DEPTH, MLSTM_WIDTH, D_MODEL), MLSTM_WIDTH ** -0.5)
    w_branch_b = nrm(ks[12], (DEPTH, LRU_WIDTH, D_MODEL), LRU_WIDTH ** -0.5)
    w_out = nrm(ks[13], (DEPTH, D_MODEL, D_MODEL), D_MODEL ** -0.5)
    final_g = 1.0 + nrm(ks[14], (D_MODEL,), 0.02)
    return {"x": x, "norm_g": norm_g, "w_in": w_in, "b_if": b_if, "head_g": head_g,
            "conv_w": conv_w, "conv_b": conv_b, "w_rg": w_rg, "b_rg": b_rg,
            "lru_lambda": lru_lambda, "w_branch_a": w_branch_a, "w_branch_b": w_branch_b,
            "w_out": w_out, "final_g": final_g}


def reference(x, norm_g, w_in, b_if, head_g, conv_w, conv_b, w_rg, b_rg, lru_lambda,
              w_branch_a, w_branch_b, w_out, final_g):
    h = x
    for layer in range(DEPTH):
        h = _hybrid_layer(h, norm_g[layer], w_in[layer], b_if[layer], head_g[layer],
                          conv_w[layer], conv_b[layer], w_rg[layer], b_rg[layer],
                          lru_lambda[layer], w_branch_a[layer], w_branch_b[layer], w_out[layer])
    return _rms_norm(h, final_g)
```

```python
import functools

import jax
import jax.numpy as jnp
import numpy as np
from jax import lax
from jax.experimental import pallas as pl
from jax.experimental.pallas import tpu as pltpu

F32 = jnp.float32
BF16 = jnp.bfloat16

D_MODEL = 2048
BATCH = 4
SEQ = 4096
DEPTH = 2
HEADS = 4
QK_DIM = 256
V_DIM = 512
QK_WIDTH = HEADS * QK_DIM
M_WIDTH = HEADS * V_DIM
N_GATE = 4 * HEADS
LRU_W = D_MODEL
LRU_BLOCK = 128
LRU_C = 8.0
CONV_W = 4
NORM_EPS = 1e-6
QK_SCALE = QK_DIM ** -0.5

_Q0, _K0, _V0, _O0, _ZA0 = 0, 1024, 2048, 4096, 6144
_G0 = 8192
_XB0 = _G0 + N_GATE
_ZB0 = _XB0 + LRU_W
_GA0 = _ZB0 + LRU_W
_GB0 = _GA0 + D_MODEL
N_IN = _GB0 + D_MODEL

MAIN_O, MAIN_ZA, MAIN_GA, MAIN_GB, MAIN_V, MAIN_Q = 0, 2048, 4096, 6144, 8192, 10240
N_MAIN = 11264

CHUNK = 256
N_CHUNK = SEQ // CHUNK
V_AUG = V_DIM + 128

NSEG = 8
SEG = SEQ // NSEG
LRU_WC = 256
LRU_TC = 16

NORM_ROWS = 128
VMEM_LIMIT = 56 * 1024 * 1024


def _cparams(n_axes):
    return pltpu.CompilerParams(dimension_semantics=("arbitrary",) * n_axes,
                                vmem_limit_bytes=VMEM_LIMIT)


def _sigmoid(x):
    return 0.5 * jnp.tanh(0.5 * x) + 0.5


def _log_sigmoid(x):
    return jnp.minimum(x, 0.0) - jnp.log1p(jnp.exp(-jnp.abs(x)))


def _softplus(x):
    return jnp.maximum(x, 0.0) + jnp.log1p(jnp.exp(-jnp.abs(x)))


def _rms_rows_to(x_ref, g_ref, hn_ref):
    tm = hn_ref.shape[0]
    g = g_ref[...]

    def body(r, carry):
        rows = pl.ds(pl.multiple_of(r * NORM_ROWS, NORM_ROWS), NORM_ROWS)
        x = x_ref[0, rows, :]
        ms = jnp.mean(x * x, axis=-1, keepdims=True)
        hn_ref[rows, :] = (x * lax.rsqrt(ms + NORM_EPS) * g).astype(BF16)
        return carry

    lax.fori_loop(0, tm // NORM_ROWS, body, 0)


def _norm_mm_kernel(x_ref, g_ref, w_ref, o_ref, hn_ref):
    @pl.when(pl.program_id(2) == 0)
    def _():
        _rms_rows_to(x_ref, g_ref, hn_ref)

    o_ref[0] = jnp.dot(hn_ref[...], w_ref[...],
                       preferred_element_type=F32).astype(o_ref.dtype)


def _norm_mm(x, g, w, out_shape, out_index, tm, tn):
    n = w.shape[1]
    return pl.pallas_call(
        _norm_mm_kernel,
        grid=(BATCH, SEQ // tm, n // tn),
        in_specs=[pl.BlockSpec((1, tm, D_MODEL), lambda b, i, j: (b, i, 0)),
                  pl.BlockSpec((1, D_MODEL), lambda b, i, j: (0, 0)),
                  pl.BlockSpec((D_MODEL, tn), lambda b, i, j: (0, j))],
        out_specs=pl.BlockSpec((1, tm, tn), out_index),
        out_shape=out_shape,
        scratch_shapes=[pltpu.VMEM((tm, D_MODEL), BF16)],
        compiler_params=_cparams(3),
    )(x, g, w)


def _chunk_cumsum(x, axis, reverse):
    n = x.shape[axis]
    idx = lax.broadcasted_iota(jnp.int32, x.shape, axis) % CHUNK
    d = 1
    while d < CHUNK:
        if reverse:
            x = x + jnp.where(idx < CHUNK - d, pltpu.roll(x, n - d, axis), 0.0)
        else:
            x = x + jnp.where(idx >= d, pltpu.roll(x, d, axis), 0.0)
        d *= 2
    return x


def _kgate_kernel(x_ref, g_ref, wkt_ref, wa_ref, wb_ref, ba_ref, bb_ref, wc_ref, bc_ref,
                  kt_ref, grow_ref, gcol_ref, hn_ref):
    _rms_rows_to(x_ref, g_ref, hn_ref)
    hn = hn_ref[...]
    nt = (((1,), (1,)), ((), ()))
    kt = lax.dot_general(wkt_ref[...], hn, nt, preferred_element_type=F32)
    kt_ref[...] = (kt * QK_SCALE).astype(BF16)

    xa = lax.dot_general(wa_ref[...], hn, nt, preferred_element_type=F32) + ba_ref[...]
    xf = lax.dot_general(wb_ref[...], hn, nt, preferred_element_type=F32) + bb_ref[...]
    lf = _log_sigmoid(xf)
    r8 = lax.broadcasted_iota(jnp.int32, lf.shape, 0) % 8
    cum = jnp.where(r8 < 2, _chunk_cumsum(lf, 1, False), _chunk_cumsum(lf, 1, True))
    rows = jnp.where((r8 == 0) | (r8 == 2), xa - cum, cum)
    rows = jnp.where(r8 < 4, rows, 0.0)
    grow_ref[...] = rows.reshape(HEADS, 8, rows.shape[1])

    xc = jnp.dot(hn, wc_ref[...], preferred_element_type=F32) + bc_ref[...]
    lfc = _log_sigmoid(xc)
    lane = lax.broadcasted_iota(jnp.int32, lfc.shape, 1)
    gcol_ref[...] = jnp.where(lane < HEADS, _chunk_cumsum(lfc, 0, False),
                              _chunk_cumsum(lfc, 0, True))


def _kgate(x, g, wkt, wa, wb, ba, bb, wc, bc, tm):
    nt = SEQ // tm
    tok = BATCH * SEQ
    full = lambda shape: pl.BlockSpec(shape, lambda b, i: (0,) * len(shape))
    return pl.pallas_call(
        _kgate_kernel,
        grid=(BATCH, nt),
        in_specs=[pl.BlockSpec((1, tm, D_MODEL), lambda b, i: (b, i, 0)),
                  full((1, D_MODEL)), full((QK_WIDTH, D_MODEL)),
                  full((8 * HEADS, D_MODEL)), full((8 * HEADS, D_MODEL)),
                  full((8 * HEADS, 1)), full((8 * HEADS, 1)),
                  full((D_MODEL, 128)), full((1, 128))],
        out_specs=[pl.BlockSpec((QK_WIDTH, tm), lambda b, i: (0, b * nt + i)),
                   pl.BlockSpec((HEADS, 8, tm), lambda b, i: (0, 0, b * nt + i)),
                   pl.BlockSpec((tm, 128), lambda b, i: (b * nt + i, 0))],
        out_shape=[jax.ShapeDtypeStruct((QK_WIDTH, tok), BF16),
                   jax.ShapeDtypeStruct((HEADS, 8, tok), F32),
                   jax.ShapeDtypeStruct((tok, 128), F32)],
        scratch_shapes=[pltpu.VMEM((tm, D_MODEL), BF16)],
        compiler_params=_cparams(2),
    )(x, g, wkt, wa, wb, ba, bb, wc, bc)


def _mlstm_kernel(qf_ref, ktf_ref, vf_ref, rowf_ref, colf_ref,
                  qb_ref, ktb_ref, vb_ref, rowb_ref, colb_ref,
                  hf_ref, hb_ref, ct_ref, m_ref):
    head = pl.program_id(1)

    @pl.when(pl.program_id(2) == 0)
    def _():
        ct_ref[...] = jnp.zeros_like(ct_ref)
        m_ref[...] = jnp.zeros_like(m_ref)

    L = CHUNK
    jj = lax.broadcasted_iota(jnp.int32, (L, L), 0)
    ss = lax.broadcasted_iota(jnp.int32, (L, L), 1)
    lane = lax.broadcasted_iota(jnp.int32, (L, 128), 1)
    ones_blk = jnp.where(lane == 0, 1.0, 0.0).astype(BF16)

    dirs = ((qf_ref, ktf_ref, vf_ref, rowf_ref, colf_ref, hf_ref),
            (qb_ref, ktb_ref, vb_ref, rowb_ref, colb_ref, hb_ref))
    for d, (q_ref, kt_ref, v_ref, row_ref, col_ref, out_ref) in enumerate(dirs):
        q = q_ref[0]
        kt = kt_ref[...]
        v_aug = jnp.concatenate([v_ref[0], ones_blk], axis=1)
        rows = row_ref[0]
        a_row = rows[2 * d:2 * d + 1, :]
        b_row = rows[2 * d + 1:2 * d + 2, :]
        b_col = jnp.sum(jnp.where(lane == head + HEADS * d, col_ref[...], 0.0),
                        axis=1, keepdims=True)
        m_prev = m_ref[d, 0:1, 0:1]

        causal = (ss <= jj) if d == 0 else (ss >= jj)
        d_log = jnp.where(causal, b_col + a_row, -jnp.inf)
        inter = b_col + m_prev
        m_row = jnp.maximum(inter, jnp.max(d_log, axis=1, keepdims=True))
        w_intra = jnp.exp(d_log - m_row)
        w_inter = jnp.exp(inter - m_row)

        s = jnp.dot(q, kt, preferred_element_type=F32)
        p = (s * w_intra).astype(BF16)
        ct = ct_ref[d]
        nd = (jnp.dot(p, v_aug, preferred_element_type=F32)
              + w_inter * jnp.dot(q, ct.astype(BF16), preferred_element_type=F32))
        num = nd[:, :V_DIM]
        den = nd[:, V_DIM:V_DIM + 1]
        out_ref[0] = num / jnp.maximum(jnp.abs(den), jnp.exp(-m_row))

        g_tot = b_row[:, L - 1:L] if d == 0 else b_row[:, 0:1]
        w_log = g_tot + a_row
        m_new = jnp.maximum(g_tot + m_prev, jnp.max(w_log, axis=1, keepdims=True))
        w_k = jnp.exp(w_log - m_new)
        decay = jnp.exp(g_tot + m_prev - m_new)
        ktw = (kt.astype(F32) * w_k).astype(BF16)
        ct_ref[d] = decay * ct + jnp.dot(ktw, v_aug, preferred_element_type=F32)
        m_ref[d] = jnp.broadcast_to(m_new, (8, 128))


def _mlstm(main, kt, grow, gcol):
    L, nc = CHUNK, N_CHUNK
    rev = lambda c: nc - 1 - c

    def specs(cmap):
        return [
            pl.BlockSpec((1, L, QK_DIM), lambda b, h, c: (b, cmap(c), MAIN_Q // QK_DIM + h)),
            pl.BlockSpec((QK_DIM, L), lambda b, h, c: (h, b * nc + cmap(c))),
            pl.BlockSpec((1, L, V_DIM), lambda b, h, c: (b, cmap(c), MAIN_V // V_DIM + h)),
            pl.BlockSpec((1, 8, L), lambda b, h, c: (h, 0, b * nc + cmap(c))),
            pl.BlockSpec((L, 128), lambda b, h, c: (b * nc + cmap(c), 0)),
        ]

    fwd = lambda c: c
    out_sds = jax.ShapeDtypeStruct((BATCH, SEQ, M_WIDTH), F32)
    return pl.pallas_call(
        _mlstm_kernel,
        grid=(BATCH, HEADS, nc),
        in_specs=specs(fwd) + specs(rev),
        out_specs=[pl.BlockSpec((1, L, V_DIM), lambda b, h, c: (b, c, h)),
                   pl.BlockSpec((1, L, V_DIM), lambda b, h, c: (b, rev(c), h))],
        out_shape=[out_sds, out_sds],
        scratch_shapes=[pltpu.VMEM((2, QK_DIM, V_AUG), F32),
                        pltpu.VMEM((2, 8, 128), F32)],
        compiler_params=_cparams(3),
    )(main, kt, main, grow, gcol, main, kt, main, grow, gcol)


def _lru_kernel(x_ref, z_ref, cw_ref, cb_ref, w_ref, b_ref, lam_ref, o_ref,
                xs_ref, h1_ref, h2_ref, p_ref):
    wc = x_ref.shape[-1]
    nb = wc // LRU_BLOCK
    tc = LRU_TC
    n_it = SEG // tc
    sub = lax.broadcasted_iota(jnp.int32, (NSEG, wc), 0)

    def copy_body(it, carry):
        t0 = pl.multiple_of(it * tc, tc)
        xs_ref[pl.ds(t0 + 2, tc)] = x_ref[0, pl.ds(t0, tc)]
        return carry

    lax.fori_loop(0, n_it, copy_body, 0)
    for r in range(2):
        prev = pltpu.roll(x_ref[0, SEG - 2 + r], 1, 0)
        xs_ref[r] = jnp.where(sub >= 1, prev, 0.0)
    nxt = pltpu.roll(x_ref[0, 0], NSEG - 1, 0)
    xs_ref[SEG + 2] = jnp.where(sub <= NSEG - 2, nxt, 0.0)

    cw = [cw_ref[t:t + 1, :][None] for t in range(CONV_W)]
    cb = cb_ref[...][None]
    ncsp = -LRU_C * _softplus(-lam_ref[...])

    def conv(t0):
        acc = xs_ref[pl.ds(t0, tc)] * cw[0]
        for t in range(1, CONV_W):
            acc = acc + xs_ref[pl.ds(t0 + t, tc)] * cw[t]
        return cb + acc

    def gates(xc, d):
        x2 = xc.reshape(tc * NSEG, wc)
        x16 = x2.astype(BF16)
        a_parts, u_parts = [], []
        for j in range(nb):
            cols = slice(j * LRU_BLOCK, (j + 1) * LRU_BLOCK)
            pre = jnp.dot(x16[:, cols], w_ref[d, j], preferred_element_type=F32) + b_ref[d, j]
            r = _sigmoid(pre[:, :LRU_BLOCK])
            i = _sigmoid(pre[:, LRU_BLOCK:])
            log_a = ncsp[d:d + 1, cols] * r
            a_parts.append(jnp.exp(log_a))
            th = jnp.tanh(log_a)
            u_parts.append(jnp.sqrt(-2.0 * th) * lax.rsqrt(1.0 - th) * (i * x2[:, cols]))
        a = jnp.concatenate(a_parts, axis=1).reshape(tc, NSEG, wc)
        u = jnp.concatenate(u_parts, axis=1).reshape(tc, NSEG, wc)
        return a, u

    def local_scan(d, h_ref):
        def body(it, carry):
            h, p = carry
            blk = it if d == 0 else n_it - 1 - it
            t0 = pl.multiple_of(blk * tc, tc)
            a, u = gates(conv(t0), d)
            hs, ps = [None] * tc, [None] * tc
            order = range(tc) if d == 0 else range(tc - 1, -1, -1)
            for k in order:
                h = a[k] * h + u[k]
                p = a[k] * p
                hs[k], ps[k] = h, p
            h_ref[pl.ds(t0, tc)] = jnp.stack(hs)
            p_ref[pl.ds(t0, tc)] = jnp.stack(ps)
            return h, p

        init = (jnp.zeros((NSEG, wc), F32), jnp.ones((NSEG, wc), F32))
        return lax.fori_loop(0, n_it, body, init)

    def carry_in(h_end, p_end, d):
        cin = jnp.zeros((NSEG, wc), F32)
        c = jnp.zeros((1, wc), F32)
        order = range(NSEG) if d == 0 else range(NSEG - 1, -1, -1)
        for s in order:
            cin = jnp.where(sub == s, c, cin)
            c = h_end[s:s + 1] + p_end[s:s + 1] * c
        return cin

    h_end, p_end = local_scan(0, h1_ref)
    cin_f = carry_in(h_end, p_end, 0)

    def fix_body(it, carry):
        rows = pl.ds(pl.multiple_of(it * tc, tc), tc)
        h1_ref[rows] = h1_ref[rows] + p_ref[rows] * cin_f
        return carry

    lax.fori_loop(0, n_it, fix_body, 0)

    h_end, p_end = local_scan(1, h2_ref)
    cin_b = carry_in(h_end, p_end, 1)

    def out_body(it, carry):
        rows = pl.ds(pl.multiple_of(it * tc, tc), tc)
        z = z_ref[0, rows]
        h = h1_ref[rows] + (h2_ref[rows] + p_ref[rows] * cin_b)
        o_ref[0, rows] = (h * (z * _sigmoid(z))).astype(o_ref.dtype)
        return carry

    lax.fori_loop(0, n_it, out_body, 0)


def _lru(xz, cw, cb, w, b, lam):
    wc = LRU_WC
    nb = wc // LRU_BLOCK
    nj = LRU_W // wc
    blk = (1, SEG, NSEG, wc)
    return pl.pallas_call(
        _lru_kernel,
        grid=(BATCH, nj),
        in_specs=[pl.BlockSpec(blk, lambda b_, j: (b_, 0, 0, j)),
                  pl.BlockSpec(blk, lambda b_, j: (b_, 0, 0, nj + j)),
                  pl.BlockSpec((CONV_W, wc), lambda b_, j: (0, j)),
                  pl.BlockSpec((1, wc), lambda b_, j: (0, j)),
                  pl.BlockSpec((2, nb, LRU_BLOCK, 2 * LRU_BLOCK), lambda b_, j: (0, j, 0, 0)),
                  pl.BlockSpec((2, nb, 1, 2 * LRU_BLOCK), lambda b_, j: (0, j, 0, 0)),
                  pl.BlockSpec((2, wc), lambda b_, j: (0, j))],
        out_specs=pl.BlockSpec(blk, lambda b_, j: (b_, 0, 0, j)),
        out_shape=jax.ShapeDtypeStruct((BATCH, SEG, NSEG, LRU_W), F32),
        scratch_shapes=[pltpu.VMEM((SEG + 3, NSEG, wc), F32),
                        pltpu.VMEM((SEG, NSEG, wc), F32),
                        pltpu.VMEM((SEG, NSEG, wc), F32),
                        pltpu.VMEM((SEG, NSEG, wc), F32)],
        compiler_params=_cparams(2),
    )(xz, xz, cw, cb, w, b, lam)


def _branch_a_kernel(hf_ref, hb_ref, o_ref, za_ref, hg_ref, ga_ref, w_ref, out_ref, ya_ref):
    @pl.when(pl.program_id(2) == 0)
    def _():
        hg = hg_ref[...]

        def body(r, carry):
            rows = pl.ds(pl.multiple_of(r * NORM_ROWS, NORM_ROWS), NORM_ROWS)
            h = hf_ref[0, rows, :] + hb_ref[0, rows, :]
            parts = []
            for hh in range(HEADS):
                hs = h[:, hh * V_DIM:(hh + 1) * V_DIM]
                ms = jnp.mean(hs * hs, axis=-1, keepdims=True)
                parts.append(hs * lax.rsqrt(ms + NORM_EPS))
            hn = jnp.concatenate(parts, axis=1) * hg
            o = o_ref[0, rows, :].astype(F32)
            z = za_ref[0, rows, :].astype(F32)
            ya_ref[rows, :] = (hn * _sigmoid(o) * (z * _sigmoid(z))).astype(BF16)
            return carry

        lax.fori_loop(0, ya_ref.shape[0] // NORM_ROWS, body, 0)

    a = jnp.dot(ya_ref[...], w_ref[...], preferred_element_type=F32)
    out_ref[0] = _sigmoid(ga_ref[0].astype(F32)) * a


def _branch_a(hf, hb, main, hg, w, tm, tn):
    wide = (1, tm, M_WIDTH)
    return pl.pallas_call(
        _branch_a_kernel,
        grid=(BATCH, SEQ // tm, D_MODEL // tn),
        in_specs=[pl.BlockSpec(wide, lambda b, i, j: (b, i, 0)),
                  pl.BlockSpec(wide, lambda b, i, j: (b, i, 0)),
                  pl.BlockSpec(wide, lambda b, i, j: (b, i, MAIN_O // M_WIDTH)),
                  pl.BlockSpec(wide, lambda b, i, j: (b, i, MAIN_ZA // M_WIDTH)),
                  pl.BlockSpec((1, M_WIDTH), lambda b, i, j: (0, 0)),
                  pl.BlockSpec((1, tm, tn), lambda b, i, j: (b, i, MAIN_GA // tn + j)),
                  pl.BlockSpec((M_WIDTH, tn), lambda b, i, j: (0, j))],
        out_specs=pl.BlockSpec((1, tm, tn), lambda b, i, j: (b, i, j)),
        out_shape=jax.ShapeDtypeStruct((BATCH, SEQ, D_MODEL), F32),
        scratch_shapes=[pltpu.VMEM((tm, M_WIDTH), BF16)],
        compiler_params=_cparams(3),
    )(hf, hb, main, main, hg, main, w)


def _branch_b_kernel(yb_ref, gb_ref, pa_ref, w_ref, out_ref, y16_ref):
    @pl.when(pl.program_id(2) == 0)
    def _():
        def body(r, carry):
            rows = pl.ds(pl.multiple_of(r * NORM_ROWS, NORM_ROWS), NORM_ROWS)
            y16_ref[rows, :] = yb_ref[0, rows, :].astype(BF16)
            return carry

        lax.fori_loop(0, y16_ref.shape[0] // NORM_ROWS, body, 0)

    b = jnp.dot(y16_ref[...], w_ref[...], preferred_element_type=F32)
    out_ref[0] = (pa_ref[0] + _sigmoid(gb_ref[0].astype(F32)) * b).astype(out_ref.dtype)


def _branch_b(yb, main, part_a, w, tn):
    tm = SEG
    return pl.pallas_call(
        _branch_b_kernel,
        grid=(BATCH, NSEG, D_MODEL // tn),
        in_specs=[pl.BlockSpec((1, tm, LRU_W), lambda b, i, j: (b, 0, i)),
                  pl.BlockSpec((1, tm, tn), lambda b, i, j: (b, i, MAIN_GB // tn + j)),
                  pl.BlockSpec((1, tm, tn), lambda b, i, j: (b, i, j)),
                  pl.BlockSpec((LRU_W, tn), lambda b, i, j: (0, j))],
        out_specs=pl.BlockSpec((1, tm, tn), lambda b, i, j: (b, i, j)),
        out_shape=jax.ShapeDtypeStruct((BATCH, SEQ, D_MODEL), BF16),
        scratch_shapes=[pltpu.VMEM((tm, LRU_W), BF16)],
        compiler_params=_cparams(3),
    )(yb, main, part_a, w)


def _out_kernel(m_ref, x_ref, w_ref, fg_ref, o_ref, *, final_norm):
    y = x_ref[0] + jnp.dot(m_ref[0], w_ref[...], preferred_element_type=F32)
    if final_norm:
        ms = jnp.mean(y * y, axis=-1, keepdims=True)
        y = y * lax.rsqrt(ms + NORM_EPS) * fg_ref[...]
    o_ref[0] = y


def _out_proj(merged, x, w, fg, final_norm, tm):
    row = lambda b, i: (b, i, 0)
    return pl.pallas_call(
        functools.partial(_out_kernel, final_norm=final_norm),
        grid=(BATCH, SEQ // tm),
        in_specs=[pl.BlockSpec((1, tm, D_MODEL), row),
                  pl.BlockSpec((1, tm, D_MODEL), row),
                  pl.BlockSpec((D_MODEL, D_MODEL), lambda b, i: (0, 0)),
                  pl.BlockSpec((1, D_MODEL), lambda b, i: (0, 0))],
        out_specs=pl.BlockSpec((1, tm, D_MODEL), row),
        out_shape=jax.ShapeDtypeStruct((BATCH, SEQ, D_MODEL), F32),
        compiler_params=_cparams(2),
    )(merged, x, w, fg)


def _gate_layouts(wg, bg):
    idx_a = np.zeros((HEADS, 8), np.int32)
    msk_a = np.zeros((HEADS, 8), np.float32)
    idx_b = np.zeros((HEADS, 8), np.int32)
    msk_b = np.zeros((HEADS, 8), np.float32)
    for h in range(HEADS):
        idx_a[h, 0], msk_a[h, 0] = h, 1.0
        idx_a[h, 2], msk_a[h, 2] = HEADS + h, 1.0
        idx_b[h, 0:2], msk_b[h, 0:2] = 2 * HEADS + h, 1.0
        idx_b[h, 2:4], msk_b[h, 2:4] = 3 * HEADS + h, 1.0
    idx_a, msk_a = idx_a.reshape(-1), msk_a.reshape(-1, 1)
    idx_b, msk_b = idx_b.reshape(-1), msk_b.reshape(-1, 1)
    wgt = wg.T
    wa = (wgt[idx_a] * msk_a).astype(BF16)
    wb = (wgt[idx_b] * msk_b).astype(BF16)
    ba = bg[idx_a][:, None] * msk_a
    bb = bg[idx_b][:, None] * msk_b
    idx_c = np.zeros((128,), np.int32)
    msk_c = np.zeros((1, 128), np.float32)
    idx_c[:2 * HEADS] = 2 * HEADS + np.arange(2 * HEADS)
    msk_c[0, :2 * HEADS] = 1.0
    wc = (wg[:, idx_c] * msk_c).astype(BF16)
    bc = bg[idx_c][None, :] * msk_c
    return wa, wb, ba, bb, wc, bc


def _layer(x, norm_g, w_in, b_if, head_g, conv_w, conv_b, w_rg, b_rg, lam,
           w_a, w_b, w_out, final_g, final_norm):
    w_main = jnp.concatenate(
        [w_in[:, _O0:_ZA0], w_in[:, _ZA0:_G0], w_in[:, _GA0:_GB0], w_in[:, _GB0:N_IN],
         w_in[:, _V0:_O0], w_in[:, _Q0:_K0]], axis=1).astype(BF16)
    w_xz = w_in[:, _XB0:_GA0].astype(BF16)
    wkt = w_in[:, _K0:_V0].T.astype(BF16)
    wa, wb, ba, bb, wc, bc = _gate_layouts(w_in[:, _G0:_XB0], b_if)
    g = norm_g[None, :]

    main = _norm_mm(x, g, w_main, jax.ShapeDtypeStruct((BATCH, SEQ, N_MAIN), BF16),
                    lambda b, i, j: (b, i, j), tm=1024, tn=1024)
    xz_tn = 1024
    xz_cols = 2 * LRU_W // xz_tn
    xz = _norm_mm(x, g, w_xz, jax.ShapeDtypeStruct((BATCH, SEG, NSEG * 2 * LRU_W), F32),
                  lambda b, i, j: (b, 0, i * xz_cols + j), tm=SEG, tn=xz_tn)
    kt, grow, gcol = _kgate(x, g, wkt, wa, wb, ba, bb, wc, bc, tm=512)

    hf, hb = _mlstm(main, kt, grow, gcol)

    w_gate = jnp.transpose(w_rg, (0, 2, 3, 1, 4)).reshape(
        2, LRU_W // LRU_BLOCK, LRU_BLOCK, 2 * LRU_BLOCK).astype(BF16)
    b_gate = jnp.transpose(b_rg.reshape(2, 2, LRU_W // LRU_BLOCK, LRU_BLOCK), (0, 2, 1, 3)).reshape(
        2, LRU_W // LRU_BLOCK, 1, 2 * LRU_BLOCK)
    yb = _lru(xz.reshape(BATCH, SEG, NSEG, 2 * LRU_W), conv_w, conv_b[None, :], w_gate, b_gate, lam)

    part_a = _branch_a(hf, hb, main, head_g[None, :], w_a.astype(BF16), tm=512, tn=1024)
    merged = _branch_b(yb.reshape(BATCH, SEG, NSEG * LRU_W), main, part_a, w_b.astype(BF16), tn=1024)
    return _out_proj(merged, x, w_out.astype(BF16), final_g[None, :], final_norm, tm=512)


def kernel(x, norm_g, w_in, b_if, head_g, conv_w, conv_b, w_rg, b_rg, lru_lambda,
           w_branch_a, w_branch_b, w_out, final_g):
    h = x
    for l in range(DEPTH):
        h = _layer(h, norm_g[l], w_in[l], b_if[l], head_g[l], conv_w[l], conv_b[l],
                   w_rg[l], b_rg[l], lru_lambda[l], w_branch_a[l], w_branch_b[l],
                   w_out[l], final_g, l == DEPTH - 1)
    return h
```

```python
import functools

import jax
import jax.numpy as jnp
import numpy as np
from jax import lax
from jax.experimental import pallas as pl
from jax.experimental.pallas import tpu as pltpu

F32 = jnp.float32
BF16 = jnp.bfloat16

D_MODEL = 2048
BATCH = 4
SEQ = 4096
DEPTH = 2
HEADS = 4
QK_DIM = 256
V_DIM = 512
QK_WIDTH = HEADS * QK_DIM
M_WIDTH = HEADS * V_DIM
N_GATE = 4 * HEADS
LRU_W = D_MODEL
LRU_BLOCK = 128
LRU_C = 8.0
CONV_W = 4
NORM_EPS = 1e-6
QK_SCALE = QK_DIM ** -0.5

_K0, _V0 = 1024, 2048
_G0 = 8192
_TAIL0 = _G0 + N_GATE
W_HALF = 8192

HEAD_O, HEAD_ZA, HEAD_V, HEAD_Q = 0, 2048, 4096, 6144
N_HEAD = 7168

CHUNK = 256
N_CHUNK = SEQ // CHUNK
V_AUG = V_DIM + 128

NSEG = 8
SEG = SEQ // NSEG
LRU_WC = 256
LRU_TC = 16

NORM_ROWS = 128
MM_TN = 1024
VMEM_LIMIT = 56 * 1024 * 1024


def _cparams(n_axes):
    return pltpu.CompilerParams(dimension_semantics=("arbitrary",) * n_axes,
                                vmem_limit_bytes=VMEM_LIMIT)


def _sigmoid(x):
    return 0.5 * jnp.tanh(0.5 * x) + 0.5


def _log_sigmoid(x):
    return jnp.minimum(x, 0.0) - jnp.log1p(jnp.exp(-jnp.abs(x)))


def _softplus(x):
    return jnp.maximum(x, 0.0) + jnp.log1p(jnp.exp(-jnp.abs(x)))


def _chunk_cumsum(x, axis, reverse):
    n = x.shape[axis]
    idx = lax.broadcasted_iota(jnp.int32, x.shape, axis) % CHUNK
    d = 1
    while d < CHUNK:
        if reverse:
            x = x + jnp.where(idx < CHUNK - d, pltpu.roll(x, n - d, axis), 0.0)
        else:
            x = x + jnp.where(idx >= d, pltpu.roll(x, d, axis), 0.0)
        d *= 2
    return x


def _kgate_kernel(x_ref, g_ref, wkt_ref, wa_ref, wb_ref, ba_ref, bb_ref, wc_ref, bc_ref,
                  hn_ref, kt_ref, grow_ref, gcol_ref):
    g = g_ref[...]

    def norm_body(r, carry):
        rows = pl.ds(pl.multiple_of(r * NORM_ROWS, NORM_ROWS), NORM_ROWS)
        x = x_ref[0, rows, :]
        ms = jnp.mean(x * x, axis=-1, keepdims=True)
        hn_ref[0, rows, :] = (x * lax.rsqrt(ms + NORM_EPS) * g).astype(BF16)
        return carry

    lax.fori_loop(0, hn_ref.shape[1] // NORM_ROWS, norm_body, 0)
    hn = hn_ref[0]
    nt = (((1,), (1,)), ((), ()))
    kt = lax.dot_general(wkt_ref[0], hn, nt, preferred_element_type=F32)
    kt_ref[...] = (kt * QK_SCALE).astype(BF16)

    xa = lax.dot_general(wa_ref[...], hn, nt, preferred_element_type=F32) + ba_ref[...]
    xf = lax.dot_general(wb_ref[...], hn, nt, preferred_element_type=F32) + bb_ref[...]
    lf = _log_sigmoid(xf)
    r8 = lax.broadcasted_iota(jnp.int32, lf.shape, 0) % 8
    cum = jnp.where(r8 < 2, _chunk_cumsum(lf, 1, False), _chunk_cumsum(lf, 1, True))
    rows = jnp.where((r8 == 0) | (r8 == 2), xa - cum, cum)
    rows = jnp.where(r8 < 4, rows, 0.0)
    grow_ref[...] = rows.reshape(HEADS, 8, rows.shape[1])

    xc = jnp.dot(hn, wc_ref[...], preferred_element_type=F32) + bc_ref[...]
    lfc = _log_sigmoid(xc)
    lane = lax.broadcasted_iota(jnp.int32, lfc.shape, 1)
    gcol_ref[...] = jnp.where(lane < HEADS, _chunk_cumsum(lfc, 0, False),
                              _chunk_cumsum(lfc, 0, True))


def _kgate(x, g, wkt, layer, wa, wb, ba, bb, wc, bc, tm):
    nt = SEQ // tm
    tok = BATCH * SEQ
    full = lambda shape: pl.BlockSpec(shape, lambda b, i: (0,) * len(shape))
    return pl.pallas_call(
        _kgate_kernel,
        grid=(BATCH, nt),
        in_specs=[pl.BlockSpec((1, tm, D_MODEL), lambda b, i: (b, i, 0)),
                  full((1, D_MODEL)),
                  pl.BlockSpec((1, QK_WIDTH, D_MODEL), lambda b, i: (layer, 0, 0)),
                  full((8 * HEADS, D_MODEL)), full((8 * HEADS, D_MODEL)),
                  full((8 * HEADS, 1)), full((8 * HEADS, 1)),
                  full((D_MODEL, 128)), full((1, 128))],
        out_specs=[pl.BlockSpec((1, tm, D_MODEL), lambda b, i: (b, i, 0)),
                   pl.BlockSpec((QK_WIDTH, tm), lambda b, i: (0, b * nt + i)),
                   pl.BlockSpec((HEADS, 8, tm), lambda b, i: (0, 0, b * nt + i)),
                   pl.BlockSpec((tm, 128), lambda b, i: (b * nt + i, 0))],
        out_shape=[jax.ShapeDtypeStruct((BATCH, SEQ, D_MODEL), BF16),
                   jax.ShapeDtypeStruct((QK_WIDTH, tok), BF16),
                   jax.ShapeDtypeStruct((HEADS, 8, tok), F32),
                   jax.ShapeDtypeStruct((tok, 128), F32)],
        compiler_params=_cparams(2),
    )(x, g, wkt, wa, wb, ba, bb, wc, bc)


def _mm_kernel(a_ref, w_ref, o_ref):
    o_ref[0] = jnp.dot(a_ref[0], w_ref[0], preferred_element_type=F32).astype(o_ref.dtype)


def _proj(hn, w, layer, w_col, n_col, out_shape, out_index, tm):
    return pl.pallas_call(
        _mm_kernel,
        grid=(n_col, BATCH, SEQ // tm),
        in_specs=[pl.BlockSpec((1, tm, D_MODEL), lambda j, b, i: (b, i, 0)),
                  pl.BlockSpec((1, D_MODEL, MM_TN), lambda j, b, i: (layer, 0, w_col(j)))],
        out_specs=pl.BlockSpec((1, tm, MM_TN), out_index),
        out_shape=out_shape,
        compiler_params=_cparams(3),
    )(hn, w)


def _mlstm_kernel(qf_ref, ktf_ref, vf_ref, rowf_ref, colf_ref,
                  qb_ref, ktb_ref, vb_ref, rowb_ref, colb_ref,
                  hf_ref, hb_ref, ct_ref, m_ref):
    head = pl.program_id(1)

    @pl.when(pl.program_id(2) == 0)
    def _():
        ct_ref[...] = jnp.zeros_like(ct_ref)
        m_ref[...] = jnp.zeros_like(m_ref)

    L = CHUNK
    jj = lax.broadcasted_iota(jnp.int32, (L, L), 0)
    ss = lax.broadcasted_iota(jnp.int32, (L, L), 1)
    lane = lax.broadcasted_iota(jnp.int32, (L, 128), 1)
    ones_blk = jnp.where(lane == 0, 1.0, 0.0).astype(BF16)

    dirs = ((qf_ref, ktf_ref, vf_ref, rowf_ref, colf_ref, hf_ref),
            (qb_ref, ktb_ref, vb_ref, rowb_ref, colb_ref, hb_ref))
    for d, (q_ref, kt_ref, v_ref, row_ref, col_ref, out_ref) in enumerate(dirs):
        q = q_ref[0]
        kt = kt_ref[...]
        v_aug = jnp.concatenate([v_ref[0], ones_blk], axis=1)
        rows = row_ref[0]
        a_row = rows[2 * d:2 * d + 1, :]
        b_row = rows[2 * d + 1:2 * d + 2, :]
        b_col = jnp.sum(jnp.where(lane == head + HEADS * d, col_ref[...], 0.0),
                        axis=1, keepdims=True)
        m_prev = m_ref[d, 0:1, 0:1]

        causal = (ss <= jj) if d == 0 else (ss >= jj)
        d_log = jnp.where(causal, b_col + a_row, -jnp.inf)
        inter = b_col + m_prev
        m_row = jnp.maximum(inter, jnp.max(d_log, axis=1, keepdims=True))
        w_intra = jnp.exp(d_log - m_row)
        w_inter = jnp.exp(inter - m_row)

        s = jnp.dot(q, kt, preferred_element_type=F32)
        p = (s * w_intra).astype(BF16)
        ct = ct_ref[d]
        nd = (jnp.dot(p, v_aug, preferred_element_type=F32)
              + w_inter * jnp.dot(q, ct.astype(BF16), preferred_element_type=F32))
        num = nd[:, :V_DIM]
        den = nd[:, V_DIM:V_DIM + 1]
        out_ref[0] = (num / jnp.maximum(jnp.abs(den), jnp.exp(-m_row))).astype(out_ref.dtype)

        g_tot = b_row[:, L - 1:L] if d == 0 else b_row[:, 0:1]
        w_log = g_tot + a_row
        m_new = jnp.maximum(g_tot + m_prev, jnp.max(w_log, axis=1, keepdims=True))
        w_k = jnp.exp(w_log - m_new)
        decay = jnp.exp(g_tot + m_prev - m_new)
        ktw = (kt.astype(F32) * w_k).astype(BF16)
        ct_ref[d] = decay * ct + jnp.dot(ktw, v_aug, preferred_element_type=F32)
        m_ref[d] = jnp.broadcast_to(m_new, (8, 128))


def _mlstm(head, kt, grow, gcol):
    L, nc = CHUNK, N_CHUNK
    rev = lambda c: nc - 1 - c

    def specs(cmap):
        return [
            pl.BlockSpec((1, L, QK_DIM), lambda b, h, c: (b, cmap(c), HEAD_Q // QK_DIM + h)),
            pl.BlockSpec((QK_DIM, L), lambda b, h, c: (h, b * nc + cmap(c))),
            pl.BlockSpec((1, L, V_DIM), lambda b, h, c: (b, cmap(c), HEAD_V // V_DIM + h)),
            pl.BlockSpec((1, 8, L), lambda b, h, c: (h, 0, b * nc + cmap(c))),
            pl.BlockSpec((L, 128), lambda b, h, c: (b * nc + cmap(c), 0)),
        ]

    fwd = lambda c: c
    out_sds = jax.ShapeDtypeStruct((BATCH, SEQ, M_WIDTH), BF16)
    return pl.pallas_call(
        _mlstm_kernel,
        grid=(BATCH, HEADS, nc),
        in_specs=specs(fwd) + specs(rev),
        out_specs=[pl.BlockSpec((1, L, V_DIM), lambda b, h, c: (b, c, h)),
                   pl.BlockSpec((1, L, V_DIM), lambda b, h, c: (b, rev(c), h))],
        out_shape=[out_sds, out_sds],
        scratch_shapes=[pltpu.VMEM((2, QK_DIM, V_AUG), F32),
                        pltpu.VMEM((2, 8, 128), F32)],
        compiler_params=_cparams(3),
    )(head, kt, head, grow, gcol, head, kt, head, grow, gcol)


def _lru_kernel(x_ref, z_ref, cw_ref, cb_ref, w_ref, b_ref, lam_ref, o_ref,
                xs_ref, hf_ref, pf_ref, hb_ref, pb_ref):
    wc = x_ref.shape[-1]
    nb = wc // LRU_BLOCK
    tc = LRU_TC
    n_it = SEG // tc
    sub = lax.broadcasted_iota(jnp.int32, (NSEG, wc), 0)

    def copy_body(it, carry):
        t0 = pl.multiple_of(it * tc, tc)
        xs_ref[pl.ds(t0 + 2, tc)] = x_ref[0, pl.ds(t0, tc)]
        return carry

    lax.fori_loop(0, n_it, copy_body, 0)
    for r in range(2):
        prev = pltpu.roll(x_ref[0, SEG - 2 + r], 1, 0)
        xs_ref[r] = jnp.where(sub >= 1, prev, 0.0)
    nxt = pltpu.roll(x_ref[0, 0], NSEG - 1, 0)
    xs_ref[SEG + 2] = jnp.where(sub <= NSEG - 2, nxt, 0.0)

    cw = [cw_ref[t:t + 1, :][None] for t in range(CONV_W)]
    cb = cb_ref[...][None]
    hcs = (-0.5 * LRU_C) * _softplus(-lam_ref[...])

    def conv(t0):
        acc = xs_ref[pl.ds(t0, tc)] * cw[0]
        for t in range(1, CONV_W):
            acc = acc + xs_ref[pl.ds(t0 + t, tc)] * cw[t]
        return cb + acc

    def gates(xc, d):
        x2 = xc.reshape(tc * NSEG, wc)
        x16 = x2.astype(BF16)
        hx = 0.5 * x2
        a_parts, u_parts = [], []
        for j in range(nb):
            cols = slice(j * LRU_BLOCK, (j + 1) * LRU_BLOCK)
            pre = jnp.dot(x16[:, cols], w_ref[d, j], preferred_element_type=F32) + b_ref[d, j]
            t_r = jnp.tanh(0.5 * pre[:, :LRU_BLOCK])
            t_i = jnp.tanh(0.5 * pre[:, LRU_BLOCK:])
            h = hcs[d:d + 1, cols]
            log_a = h * t_r + h
            a_parts.append(jnp.exp(log_a))
            th = jnp.tanh(log_a)
            p = -2.0 * th
            q = 1.0 - th
            coef = jnp.where(p > 0.0, p * lax.rsqrt(p * q), 0.0)
            u_parts.append(coef * ((t_i + 1.0) * hx[:, cols]))
        a = jnp.concatenate(a_parts, axis=1).reshape(tc, NSEG, wc)
        u = jnp.concatenate(u_parts, axis=1).reshape(tc, NSEG, wc)
        return a, u

    def scan_body(it, carry):
        out = []
        for d, (h_ref, p_ref) in enumerate(((hf_ref, pf_ref), (hb_ref, pb_ref))):
            h, p = carry[2 * d], carry[2 * d + 1]
            blk = it if d == 0 else n_it - 1 - it
            t0 = pl.multiple_of(blk * tc, tc)
            a, u = gates(conv(t0), d)
            hs, ps = [None] * tc, [None] * tc
            order = range(tc) if d == 0 else range(tc - 1, -1, -1)
            for k in order:
                h = a[k] * h + u[k]
                p = a[k] * p
                hs[k], ps[k] = h, p
            h_ref[pl.ds(t0, tc)] = jnp.stack(hs)
            p_ref[pl.ds(t0, tc)] = jnp.stack(ps)
            out += [h, p]
        return tuple(out)

    zeros = jnp.zeros((NSEG, wc), F32)
    ones = jnp.ones((NSEG, wc), F32)
    hf_end, pf_end, hb_end, pb_end = lax.fori_loop(0, n_it, scan_body, (zeros, ones, zeros, ones))

    def carry_in(h_end, p_end, d):
        cin = jnp.zeros((NSEG, wc), F32)
        c = jnp.zeros((1, wc), F32)
        order = range(NSEG) if d == 0 else range(NSEG - 1, -1, -1)
        for s in order:
            cin = jnp.where(sub == s, c, cin)
            c = h_end[s:s + 1] + p_end[s:s + 1] * c
        return cin

    cin_f = carry_in(hf_end, pf_end, 0)
    cin_b = carry_in(hb_end, pb_end, 1)

    def out_body(it, carry):
        rows = pl.ds(pl.multiple_of(it * tc, tc), tc)
        z = z_ref[0, rows]
        h = (hf_ref[rows] + pf_ref[rows] * cin_f) + (hb_ref[rows] + pb_ref[rows] * cin_b)
        o_ref[0, rows] = (h * (z * _sigmoid(z))).astype(o_ref.dtype)
        return carry

    lax.fori_loop(0, n_it, out_body, 0)


def _lru(xz, cw, cb, w, b, lam):
    wc = LRU_WC
    nb = wc // LRU_BLOCK
    nj = LRU_W // wc
    blk = (1, SEG, NSEG, wc)
    seg_buf = pltpu.VMEM((SEG, NSEG, wc), F32)
    return pl.pallas_call(
        _lru_kernel,
        grid=(BATCH, nj),
        in_specs=[pl.BlockSpec(blk, lambda b_, j: (b_, 0, 0, j)),
                  pl.BlockSpec(blk, lambda b_, j: (b_, 0, 0, nj + j)),
                  pl.BlockSpec((CONV_W, wc), lambda b_, j: (0, j)),
                  pl.BlockSpec((1, wc), lambda b_, j: (0, j)),
                  pl.BlockSpec((2, nb, LRU_BLOCK, 2 * LRU_BLOCK), lambda b_, j: (0, j, 0, 0)),
                  pl.BlockSpec((2, nb, 1, 2 * LRU_BLOCK), lambda b_, j: (0, j, 0, 0)),
                  pl.BlockSpec((2, wc), lambda b_, j: (0, j))],
        out_specs=pl.BlockSpec(blk, lambda b_, j: (b_, 0, 0, j)),
        out_shape=jax.ShapeDtypeStruct((BATCH, SEG, NSEG, LRU_W), F32),
        scratch_shapes=[pltpu.VMEM((SEG + 3, NSEG, wc), F32),
                        seg_buf, seg_buf, seg_buf, seg_buf],
        compiler_params=_cparams(2),
    )(xz, xz, cw, cb, w, b, lam)


def _branch_a_kernel(hf_ref, hb_ref, o_ref, za_ref, hg_ref, ga_ref, w_ref, out_ref, ya_ref):
    @pl.when(pl.program_id(2) == 0)
    def _():
        hg = hg_ref[...]

        def body(r, carry):
            rows = pl.ds(pl.multiple_of(r * NORM_ROWS, NORM_ROWS), NORM_ROWS)
            h = hf_ref[0, rows, :].astype(F32) + hb_ref[0, rows, :].astype(F32)
            parts = []
            for hh in range(HEADS):
                hs = h[:, hh * V_DIM:(hh + 1) * V_DIM]
                ms = jnp.mean(hs * hs, axis=-1, keepdims=True)
                parts.append(hs * lax.rsqrt(ms + NORM_EPS))
            hn = jnp.concatenate(parts, axis=1) * hg
            o = o_ref[0, rows, :].astype(F32)
            z = za_ref[0, rows, :].astype(F32)
            ya_ref[rows, :] = (hn * _sigmoid(o) * (z * _sigmoid(z))).astype(BF16)
            return carry

        lax.fori_loop(0, ya_ref.shape[0] // NORM_ROWS, body, 0)

    a = jnp.dot(ya_ref[...], w_ref[0], preferred_element_type=F32)
    out_ref[0] = (_sigmoid(ga_ref[0].astype(F32)) * a).astype(out_ref.dtype)


def _branch_a(hf, hb, head, gg, hg, w, layer, tm, tn):
    wide = (1, tm, M_WIDTH)
    return pl.pallas_call(
        _branch_a_kernel,
        grid=(BATCH, SEQ // tm, D_MODEL // tn),
        in_specs=[pl.BlockSpec(wide, lambda b, i, j: (b, i, 0)),
                  pl.BlockSpec(wide, lambda b, i, j: (b, i, 0)),
                  pl.BlockSpec(wide, lambda b, i, j: (b, i, HEAD_O // M_WIDTH)),
                  pl.BlockSpec(wide, lambda b, i, j: (b, i, HEAD_ZA // M_WIDTH)),
                  pl.BlockSpec((1, M_WIDTH), lambda b, i, j: (0, 0)),
                  pl.BlockSpec((1, tm, tn), lambda b, i, j: (b, i, j)),
                  pl.BlockSpec((1, M_WIDTH, tn), lambda b, i, j: (layer, 0, j))],
        out_specs=pl.BlockSpec((1, tm, tn), lambda b, i, j: (b, i, j)),
        out_shape=jax.ShapeDtypeStruct((BATCH, SEQ, D_MODEL), BF16),
        scratch_shapes=[pltpu.VMEM((tm, M_WIDTH), BF16)],
        compiler_params=_cparams(3),
    )(hf, hb, head, head, hg, gg, w)


def _branch_b_kernel(yb_ref, gb_ref, pa_ref, w_ref, out_ref, y16_ref):
    @pl.when(pl.program_id(2) == 0)
    def _():
        def body(r, carry):
            rows = pl.ds(pl.multiple_of(r * NORM_ROWS, NORM_ROWS), NORM_ROWS)
            y16_ref[rows, :] = yb_ref[0, rows, :].astype(BF16)
            return carry

        lax.fori_loop(0, y16_ref.shape[0] // NORM_ROWS, body, 0)

    b = jnp.dot(y16_ref[...], w_ref[0], preferred_element_type=F32)
    out_ref[0] = (pa_ref[0].astype(F32)
                  + _sigmoid(gb_ref[0].astype(F32)) * b).astype(out_ref.dtype)


def _branch_b(yb, gg, part_a, w, layer, tn):
    tm = SEG
    return pl.pallas_call(
        _branch_b_kernel,
        grid=(BATCH, NSEG, D_MODEL // tn),
        in_specs=[pl.BlockSpec((1, tm, LRU_W), lambda b, i, j: (b, 0, i)),
                  pl.BlockSpec((1, tm, tn), lambda b, i, j: (b, i, D_MODEL // tn + j)),
                  pl.BlockSpec((1, tm, tn), lambda b, i, j: (b, i, j)),
                  pl.BlockSpec((1, LRU_W, tn), lambda b, i, j: (layer, 0, j))],
        out_specs=pl.BlockSpec((1, tm, tn), lambda b, i, j: (b, i, j)),
        out_shape=jax.ShapeDtypeStruct((BATCH, SEQ, D_MODEL), BF16),
        scratch_shapes=[pltpu.VMEM((tm, LRU_W), BF16)],
        compiler_params=_cparams(3),
    )(yb, gg, part_a, w)


def _out_kernel(m_ref, x_ref, w_ref, fg_ref, o_ref, *, final_norm):
    y = x_ref[0] + jnp.dot(m_ref[0], w_ref[0], preferred_element_type=F32)
    if final_norm:
        ms = jnp.mean(y * y, axis=-1, keepdims=True)
        y = y * lax.rsqrt(ms + NORM_EPS) * fg_ref[...]
    o_ref[0] = y


def _out_proj(merged, x, w, layer, fg, final_norm, tm):
    row = lambda b, i: (b, i, 0)
    return pl.pallas_call(
        functools.partial(_out_kernel, final_norm=final_norm),
        grid=(BATCH, SEQ // tm),
        in_specs=[pl.BlockSpec((1, tm, D_MODEL), row),
                  pl.BlockSpec((1, tm, D_MODEL), row),
                  pl.BlockSpec((1, D_MODEL, D_MODEL), lambda b, i: (layer, 0, 0)),
                  pl.BlockSpec((1, D_MODEL), lambda b, i: (0, 0))],
        out_specs=pl.BlockSpec((1, tm, D_MODEL), row),
        out_shape=jax.ShapeDtypeStruct((BATCH, SEQ, D_MODEL), F32),
        compiler_params=_cparams(2),
    )(merged, x, w, fg)


def _gate_layouts(wg, bg):
    idx_a = np.zeros((HEADS, 8), np.int32)
    msk_a = np.zeros((HEADS, 8), np.float32)
    idx_b = np.zeros((HEADS, 8), np.int32)
    msk_b = np.zeros((HEADS, 8), np.float32)
    for h in range(HEADS):
        idx_a[h, 0], msk_a[h, 0] = h, 1.0
        idx_a[h, 2], msk_a[h, 2] = HEADS + h, 1.0
        idx_b[h, 0:2], msk_b[h, 0:2] = 2 * HEADS + h, 1.0
        idx_b[h, 2:4], msk_b[h, 2:4] = 3 * HEADS + h, 1.0
    idx_a, msk_a = idx_a.reshape(-1), msk_a.reshape(-1, 1)
    idx_b, msk_b = idx_b.reshape(-1), msk_b.reshape(-1, 1)
    wgt = jnp.transpose(wg, (0, 2, 1))
    wa = (wgt[:, idx_a] * msk_a).astype(BF16)
    wb = (wgt[:, idx_b] * msk_b).astype(BF16)
    ba = bg[:, idx_a][:, :, None] * msk_a
    bb = bg[:, idx_b][:, :, None] * msk_b
    idx_c = np.zeros((128,), np.int32)
    msk_c = np.zeros((1, 128), np.float32)
    idx_c[:2 * HEADS] = 2 * HEADS + np.arange(2 * HEADS)
    msk_c[0, :2 * HEADS] = 1.0
    wc = (wg[:, :, idx_c] * msk_c).astype(BF16)
    bc = bg[:, idx_c][:, None, :] * msk_c
    return wa, wb, ba, bb, wc, bc


def kernel(x, norm_g, w_in, b_if, head_g, conv_w, conv_b, w_rg, b_rg, lru_lambda,
           w_branch_a, w_branch_b, w_out, final_g):
    w_head = w_in[:, :, :W_HALF].astype(BF16)
    w_tail = w_in[:, :, _TAIL0:].astype(BF16)
    wkt = jnp.transpose(w_in[:, :, _K0:_V0], (0, 2, 1)).astype(BF16)
    wa, wb, ba, bb, wc, bc = _gate_layouts(w_in[:, :, _G0:_TAIL0], b_if)
    n_blk = LRU_W // LRU_BLOCK
    w_gate = jnp.transpose(w_rg, (0, 1, 3, 4, 2, 5)).reshape(
        DEPTH, 2, n_blk, LRU_BLOCK, 2 * LRU_BLOCK).astype(BF16)
    b_gate = jnp.transpose(b_rg.reshape(DEPTH, 2, 2, n_blk, LRU_BLOCK), (0, 1, 3, 2, 4)).reshape(
        DEPTH, 2, n_blk, 1, 2 * LRU_BLOCK)
    w_a16 = w_branch_a.astype(BF16)
    w_b16 = w_branch_b.astype(BF16)
    w_o16 = w_out.astype(BF16)

    head_col = lambda j: jnp.where(j < 4, j + 4, jnp.where(j < 6, j - 2, 0))
    xz_cols = 2 * LRU_W // MM_TN

    h = x
    for l in range(DEPTH):
        hn, kt, grow, gcol = _kgate(h, norm_g[l][None, :], wkt, l, wa[l], wb[l], ba[l], bb[l],
                                    wc[l], bc[l], tm=512)
        head = _proj(hn, w_head, l, head_col, N_HEAD // MM_TN,
                     jax.ShapeDtypeStruct((BATCH, SEQ, N_HEAD), BF16),
                     lambda j, b, i: (b, i, j), tm=1024)
        gg = _proj(hn, w_tail, l, lambda j: xz_cols + j, 2 * D_MODEL // MM_TN,
                   jax.ShapeDtypeStruct((BATCH, SEQ, 2 * D_MODEL), BF16),
                   lambda j, b, i: (b, i, j), tm=1024)
        xz = _proj(hn, w_tail, l, lambda j: j, xz_cols,
                   jax.ShapeDtypeStruct((BATCH, SEG, NSEG * 2 * LRU_W), F32),
                   lambda j, b, i: (b, 0, i * xz_cols + j), tm=SEG)

        hf, hb = _mlstm(head, kt, grow, gcol)
        yb = _lru(xz.reshape(BATCH, SEG, NSEG, 2 * LRU_W), conv_w[l], conv_b[l][None, :],
                  w_gate[l], b_gate[l], lru_lambda[l])

        part_a = _branch_a(hf, hb, head, gg, head_g[l][None, :], w_a16, l, tm=512, tn=1024)
        merged = _branch_b(yb.reshape(BATCH, SEG, NSEG * LRU_W), gg, part_a, w_b16, l, tn=1024)
        h = _out_proj(merged, h, w_o16, l, final_g[None, :], l == DEPTH - 1, tm=512)
    return h
```

```python
import functools

import jax
import jax.numpy as jnp
import numpy as np
from jax import lax
from jax.experimental import pallas as pl
from jax.experimental.pallas import tpu as pltpu

F32 = jnp.float32
BF16 = jnp.bfloat16

D_MODEL = 2048
BATCH = 4
SEQ = 4096
DEPTH = 2
HEADS = 4
QK_DIM = 256
V_DIM = 512
QK_WIDTH = HEADS * QK_DIM
M_WIDTH = HEADS * V_DIM
N_GATE = 4 * HEADS
LRU_W = D_MODEL
LRU_BLOCK = 128
LRU_C = 8.0
CONV_W = 4
NORM_EPS = 1e-6
QK_SCALE = QK_DIM ** -0.5

_K0, _V0 = 1024, 2048
_G0 = 8192
_TAIL0 = _G0 + N_GATE

HEAD_O, HEAD_ZA, HEAD_V, HEAD_Q = 0, 2048, 4096, 6144
N_HEAD = 7168

CHUNK = 256
N_CHUNK = SEQ // CHUNK
V_AUG = V_DIM + 128

NSEG = 8
SEG = SEQ // NSEG
LRU_WC = 256
LRU_TC = 16

SEG_TILE = 64
KG_ROWS = QK_WIDTH + 16 * HEADS

NORM_ROWS = 128
MM_TN = 1024
PROJ_TM = 1024
VMEM_LIMIT = 56 * 1024 * 1024


def _cparams(n_axes):
    return pltpu.CompilerParams(dimension_semantics=("arbitrary",) * n_axes,
                                vmem_limit_bytes=VMEM_LIMIT)


def _sigmoid(x):
    return 0.5 * jnp.tanh(0.5 * x) + 0.5


def _log_sigmoid(x):
    return jnp.minimum(x, 0.0) - jnp.log1p(jnp.exp(-jnp.abs(x)))


def _softplus(x):
    return jnp.maximum(x, 0.0) + jnp.log1p(jnp.exp(-jnp.abs(x)))


def _chunk_cumsum(x, axis, reverse):
    n = x.shape[axis]
    idx = lax.broadcasted_iota(jnp.int32, x.shape, axis) % CHUNK
    d = 1
    while d < CHUNK:
        if reverse:
            x = x + jnp.where(idx < CHUNK - d, pltpu.roll(x, n - d, axis), 0.0)
        else:
            x = x + jnp.where(idx >= d, pltpu.roll(x, d, axis), 0.0)
        d *= 2
    return x


def _kgate_kernel(x_ref, g_ref, wk_ref, ba_ref, bb_ref, wc_ref, bc_ref,
                  hn_ref, kt_ref, grow_ref, gcol_ref):
    g = g_ref[...]

    def norm_body(r, carry):
        rows = pl.ds(pl.multiple_of(r * NORM_ROWS, NORM_ROWS), NORM_ROWS)
        x = x_ref[0, rows, :]
        ms = jnp.mean(x * x, axis=-1, keepdims=True)
        hn_ref[0, rows, :] = (x * lax.rsqrt(ms + NORM_EPS) * g).astype(BF16)
        return carry

    lax.fori_loop(0, hn_ref.shape[1] // NORM_ROWS, norm_body, 0)
    hn = hn_ref[0]
    nt = (((1,), (1,)), ((), ()))
    kg = lax.dot_general(wk_ref[0], hn, nt, preferred_element_type=F32)
    kt_ref[...] = (kg[:QK_WIDTH] * QK_SCALE).astype(BF16)

    xa = kg[QK_WIDTH:QK_WIDTH + 8 * HEADS] + ba_ref[...]
    xf = kg[QK_WIDTH + 8 * HEADS:] + bb_ref[...]
    lf = _log_sigmoid(xf)
    r8 = lax.broadcasted_iota(jnp.int32, lf.shape, 0) % 8
    cum = jnp.where(r8 < 2, _chunk_cumsum(lf, 1, False), _chunk_cumsum(lf, 1, True))
    rows = jnp.where((r8 == 0) | (r8 == 2), xa - cum, cum)
    rows = jnp.where(r8 < 4, rows, 0.0)
    grow_ref[...] = rows.reshape(HEADS, 8, rows.shape[1])

    xc = jnp.dot(hn, wc_ref[...], preferred_element_type=F32) + bc_ref[...]
    lfc = _log_sigmoid(xc)
    lane = lax.broadcasted_iota(jnp.int32, lfc.shape, 1)
    gcol_ref[...] = jnp.where(lane < HEADS, _chunk_cumsum(lfc, 0, False),
                              _chunk_cumsum(lfc, 0, True))


def _kgate(x, g, wk, layer, ba, bb, wc, bc, tm):
    nt = SEQ // tm
    tok = BATCH * SEQ
    full = lambda shape: pl.BlockSpec(shape, lambda b, i: (0,) * len(shape))
    return pl.pallas_call(
        _kgate_kernel,
        grid=(BATCH, nt),
        in_specs=[pl.BlockSpec((1, tm, D_MODEL), lambda b, i: (b, i, 0)),
                  full((1, D_MODEL)),
                  pl.BlockSpec((1, KG_ROWS, D_MODEL), lambda b, i: (layer, 0, 0)),
                  full((8 * HEADS, 1)), full((8 * HEADS, 1)),
                  full((D_MODEL, 128)), full((1, 128))],
        out_specs=[pl.BlockSpec((1, tm, D_MODEL), lambda b, i: (b, i, 0)),
                   pl.BlockSpec((QK_WIDTH, tm), lambda b, i: (0, b * nt + i)),
                   pl.BlockSpec((HEADS, 8, tm), lambda b, i: (0, 0, b * nt + i)),
                   pl.BlockSpec((tm, 128), lambda b, i: (b * nt + i, 0))],
        out_shape=[jax.ShapeDtypeStruct((BATCH, SEQ, D_MODEL), BF16),
                   jax.ShapeDtypeStruct((QK_WIDTH, tok), BF16),
                   jax.ShapeDtypeStruct((HEADS, 8, tok), F32),
                   jax.ShapeDtypeStruct((tok, 128), F32)],
        compiler_params=_cparams(2),
    )(x, g, wk, ba, bb, wc, bc)


W_CAST_ROWS = 256


def _proj_kernel(*refs, shift, segmented):
    if shift:
        a_ref, w_ref, w2_ref, o_ref, w16_ref = refs
    else:
        a_ref, w_ref, o_ref, w16_ref = refs

    @pl.when((pl.program_id(1) == 0) & (pl.program_id(2) == 0))
    def _():
        def body(r, carry):
            rows = pl.ds(pl.multiple_of(r * W_CAST_ROWS, W_CAST_ROWS), W_CAST_ROWS)
            w = w_ref[0, rows, :]
            if shift:
                w = jnp.concatenate([w, w2_ref[0, rows, :]], axis=1)[:, shift:shift + MM_TN]
            w16_ref[rows, :] = w.astype(BF16)
            return carry

        lax.fori_loop(0, D_MODEL // W_CAST_ROWS, body, 0)

    if segmented:
        a = a_ref[0].reshape(NSEG * SEG_TILE, D_MODEL)
        r = jnp.dot(a, w16_ref[...], preferred_element_type=F32)
        o_ref[0] = jnp.swapaxes(r.reshape(NSEG, SEG_TILE, MM_TN), 0, 1).astype(o_ref.dtype)
    else:
        o_ref[0] = jnp.dot(a_ref[0], w16_ref[...],
                           preferred_element_type=F32).astype(o_ref.dtype)


def _proj(hn, w_in, layer, col0, w_col, n_col, out_dtype, segmented=False):
    shift = col0 % 128
    base = col0 // 128
    per = MM_TN // 128
    w_specs = [pl.BlockSpec((1, D_MODEL, MM_TN),
                            lambda j, b, i: (layer, 0, base // per + w_col(j)))]
    operands = [w_in]
    if shift:
        w_specs.append(pl.BlockSpec((1, D_MODEL, 128),
                                    lambda j, b, i: (layer, 0, base + (w_col(j) + 1) * per)))
        operands.append(w_in)
    if segmented:
        a = hn.reshape(BATCH, NSEG, SEG, D_MODEL)
        grid = (n_col, BATCH, SEG // SEG_TILE)
        a_spec = pl.BlockSpec((1, NSEG, SEG_TILE, D_MODEL), lambda j, b, i: (b, 0, i, 0))
        out_spec = pl.BlockSpec((1, SEG_TILE, NSEG, MM_TN), lambda j, b, i: (b, i, 0, j))
        out_shape = jax.ShapeDtypeStruct((BATCH, SEG, NSEG, n_col * MM_TN), out_dtype)
    else:
        a = hn
        grid = (n_col, BATCH, SEQ // PROJ_TM)
        a_spec = pl.BlockSpec((1, PROJ_TM, D_MODEL), lambda j, b, i: (b, i, 0))
        out_spec = pl.BlockSpec((1, PROJ_TM, MM_TN), lambda j, b, i: (b, i, j))
        out_shape = jax.ShapeDtypeStruct((BATCH, SEQ, n_col * MM_TN), out_dtype)
    return pl.pallas_call(
        functools.partial(_proj_kernel, shift=shift, segmented=segmented),
        grid=grid,
        in_specs=[a_spec] + w_specs,
        out_specs=out_spec,
        out_shape=out_shape,
        scratch_shapes=[pltpu.VMEM((D_MODEL, MM_TN), BF16)],
        compiler_params=_cparams(3),
    )(a, *operands)


def _mlstm_kernel(qf_ref, ktf_ref, vf_ref, rowf_ref, colf_ref,
                  qb_ref, ktb_ref, vb_ref, rowb_ref, colb_ref,
                  hf_ref, hb_ref, ct_ref, m_ref):
    head = pl.program_id(1)

    @pl.when(pl.program_id(2) == 0)
    def _():
        ct_ref[...] = jnp.zeros_like(ct_ref)
        m_ref[...] = jnp.zeros_like(m_ref)

    L = CHUNK
    jj = lax.broadcasted_iota(jnp.int32, (L, L), 0)
    ss = lax.broadcasted_iota(jnp.int32, (L, L), 1)
    lane = lax.broadcasted_iota(jnp.int32, (L, 128), 1)
    ones_blk = jnp.where(lane == 0, 1.0, 0.0).astype(BF16)

    dirs = ((qf_ref, ktf_ref, vf_ref, rowf_ref, colf_ref, hf_ref),
            (qb_ref, ktb_ref, vb_ref, rowb_ref, colb_ref, hb_ref))
    for d, (q_ref, kt_ref, v_ref, row_ref, col_ref, out_ref) in enumerate(dirs):
        q = q_ref[0]
        kt = kt_ref[...]
        v_aug = jnp.concatenate([v_ref[0], ones_blk], axis=1)
        rows = row_ref[0]
        a_row = rows[2 * d:2 * d + 1, :]
        b_row = rows[2 * d + 1:2 * d + 2, :]
        b_col = jnp.sum(jnp.where(lane == head + HEADS * d, col_ref[...], 0.0),
                        axis=1, keepdims=True)
        m_prev = m_ref[d, 0:1, 0:1]

        causal = (ss <= jj) if d == 0 else (ss >= jj)
        d_log = jnp.where(causal, b_col + a_row, -jnp.inf)
        inter = b_col + m_prev
        m_row = jnp.maximum(inter, jnp.max(d_log, axis=1, keepdims=True))
        w_intra = jnp.exp(d_log - m_row)
        w_inter = jnp.exp(inter - m_row)

        s = jnp.dot(q, kt, preferred_element_type=F32)
        p = (s * w_intra).astype(BF16)
        ct = ct_ref[d]
        nd = (jnp.dot(p, v_aug, preferred_element_type=F32)
              + w_inter * jnp.dot(q, ct.astype(BF16), preferred_element_type=F32))
        num = nd[:, :V_DIM]
        den = nd[:, V_DIM:V_DIM + 1]
        out_ref[0] = (num / jnp.maximum(jnp.abs(den), jnp.exp(-m_row))).astype(out_ref.dtype)

        g_tot = b_row[:, L - 1:L] if d == 0 else b_row[:, 0:1]
        w_log = g_tot + a_row
        m_new = jnp.maximum(g_tot + m_prev, jnp.max(w_log, axis=1, keepdims=True))
        w_k = jnp.exp(w_log - m_new)
        decay = jnp.exp(g_tot + m_prev - m_new)
        ktw = (kt.astype(F32) * w_k).astype(BF16)
        ct_ref[d] = decay * ct + jnp.dot(ktw, v_aug, preferred_element_type=F32)
        m_ref[d] = jnp.broadcast_to(m_new, (8, 128))


def _mlstm(head, kt, grow, gcol):
    L, nc = CHUNK, N_CHUNK
    rev = lambda c: nc - 1 - c

    def specs(cmap):
        return [
            pl.BlockSpec((1, L, QK_DIM), lambda b, h, c: (b, cmap(c), HEAD_Q // QK_DIM + h)),
            pl.BlockSpec((QK_DIM, L), lambda b, h, c: (h, b * nc + cmap(c))),
            pl.BlockSpec((1, L, V_DIM), lambda b, h, c: (b, cmap(c), HEAD_V // V_DIM + h)),
            pl.BlockSpec((1, 8, L), lambda b, h, c: (h, 0, b * nc + cmap(c))),
            pl.BlockSpec((L, 128), lambda b, h, c: (b * nc + cmap(c), 0)),
        ]

    fwd = lambda c: c
    out_sds = jax.ShapeDtypeStruct((BATCH, SEQ, M_WIDTH), BF16)
    return pl.pallas_call(
        _mlstm_kernel,
        grid=(BATCH, HEADS, nc),
        in_specs=specs(fwd) + specs(rev),
        out_specs=[pl.BlockSpec((1, L, V_DIM), lambda b, h, c: (b, c, h)),
                   pl.BlockSpec((1, L, V_DIM), lambda b, h, c: (b, rev(c), h))],
        out_shape=[out_sds, out_sds],
        scratch_shapes=[pltpu.VMEM((2, QK_DIM, V_AUG), F32),
                        pltpu.VMEM((2, 8, 128), F32)],
        compiler_params=_cparams(3),
    )(head, kt, head, grow, gcol, head, kt, head, grow, gcol)


def _lru_kernel(x_ref, z_ref, cw_ref, cb_ref, w_ref, b_ref, lam_ref, o_ref,
                xs_ref, hf_ref, pf_ref, hb_ref, pb_ref, wh_ref):
    wc = x_ref.shape[-1]
    nb = wc // LRU_BLOCK
    tc = LRU_TC
    n_it = SEG // tc
    sub = lax.broadcasted_iota(jnp.int32, (NSEG, wc), 0)

    def copy_body(it, carry):
        t0 = pl.multiple_of(it * tc, tc)
        xs_ref[pl.ds(t0 + 2, tc)] = x_ref[0, pl.ds(t0, tc)]
        return carry

    lax.fori_loop(0, n_it, copy_body, 0)
    for r in range(2):
        prev = pltpu.roll(x_ref[0, SEG - 2 + r], 1, 0)
        xs_ref[r] = jnp.where(sub >= 1, prev, 0.0)
    nxt = pltpu.roll(x_ref[0, 0], NSEG - 1, 0)
    xs_ref[SEG + 2] = jnp.where(sub <= NSEG - 2, nxt, 0.0)

    cw = [cw_ref[t:t + 1, :][None] for t in range(CONV_W)]
    cb = cb_ref[...][None]

    def conv_body(it, carry):
        t0 = pl.multiple_of(it * tc, tc)
        acc = xs_ref[pl.ds(t0, tc)] * cw[0]
        for t in range(1, CONV_W):
            acc = acc + xs_ref[pl.ds(t0 + t, tc)] * cw[t]
        xs_ref[pl.ds(t0, tc)] = cb + acc
        return carry

    lax.fori_loop(0, n_it, conv_body, 0)

    wh_ref[...] = (0.5 * w_ref[...].astype(F32)).astype(BF16)
    bh = 0.5 * b_ref[...]
    hcs = (-0.5 * LRU_C) * _softplus(-lam_ref[...])

    def conv(t0):
        return xs_ref[pl.ds(t0, tc)]

    def gates(xc, d):
        x2 = xc.reshape(tc * NSEG, wc)
        x16 = x2.astype(BF16)
        hx = 0.5 * x2
        a_parts, u_parts = [], []
        for j in range(nb):
            cols = slice(j * LRU_BLOCK, (j + 1) * LRU_BLOCK)
            pre = jnp.dot(x16[:, cols], wh_ref[d, j], preferred_element_type=F32) + bh[d, j]
            t_r = jnp.tanh(pre[:, :LRU_BLOCK])
            t_i = jnp.tanh(pre[:, LRU_BLOCK:])
            h = hcs[d:d + 1, cols]
            log_a = h * t_r + h
            a_parts.append(jnp.exp(log_a))
            th = jnp.tanh(log_a)
            p = -2.0 * th
            q = 1.0 - th
            coef = jnp.where(p > 0.0, p * lax.rsqrt(p * q), 0.0)
            u_parts.append(coef * ((t_i + 1.0) * hx[:, cols]))
        a = jnp.concatenate(a_parts, axis=1).reshape(tc, NSEG, wc)
        u = jnp.concatenate(u_parts, axis=1).reshape(tc, NSEG, wc)
        return a, u

    def scan_body(it, carry):
        out = []
        for d, (h_ref, p_ref) in enumerate(((hf_ref, pf_ref), (hb_ref, pb_ref))):
            h, p = carry[2 * d], carry[2 * d + 1]
            blk = it if d == 0 else n_it - 1 - it
            t0 = pl.multiple_of(blk * tc, tc)
            a, u = gates(conv(t0), d)
            hs, ps = [None] * tc, [None] * tc
            order = range(tc) if d == 0 else range(tc - 1, -1, -1)
            for k in order:
                h = a[k] * h + u[k]
                p = a[k] * p
                hs[k], ps[k] = h, p
            h_ref[pl.ds(t0, tc)] = jnp.stack(hs)
            p_ref[pl.ds(t0, tc)] = jnp.stack(ps)
            out += [h, p]
        return tuple(out)

    zeros = jnp.zeros((NSEG, wc), F32)
    ones = jnp.ones((NSEG, wc), F32)
    hf_end, pf_end, hb_end, pb_end = lax.fori_loop(0, n_it, scan_body, (zeros, ones, zeros, ones))

    def carry_in(h_end, p_end, d):
        cin = jnp.zeros((NSEG, wc), F32)
        c = jnp.zeros((1, wc), F32)
        order = range(NSEG) if d == 0 else range(NSEG - 1, -1, -1)
        for s in order:
            cin = jnp.where(sub == s, c, cin)
            c = h_end[s:s + 1] + p_end[s:s + 1] * c
        return cin

    cin_f = carry_in(hf_end, pf_end, 0)
    cin_b = carry_in(hb_end, pb_end, 1)

    def out_body(it, carry):
        rows = pl.ds(pl.multiple_of(it * tc, tc), tc)
        z = z_ref[0, rows]
        h = (hf_ref[rows] + pf_ref[rows] * cin_f) + (hb_ref[rows] + pb_ref[rows] * cin_b)
        o_ref[0, rows] = (h * (z * _sigmoid(z))).astype(o_ref.dtype)
        return carry

    lax.fori_loop(0, n_it, out_body, 0)


def _lru(xz, cw, cb, w, b, lam):
    wc = LRU_WC
    nb = wc // LRU_BLOCK
    nj = LRU_W // wc
    blk = (1, SEG, NSEG, wc)
    seg_buf = pltpu.VMEM((SEG, NSEG, wc), F32)
    return pl.pallas_call(
        _lru_kernel,
        grid=(BATCH, nj),
        in_specs=[pl.BlockSpec(blk, lambda b_, j: (b_, 0, 0, j)),
                  pl.BlockSpec(blk, lambda b_, j: (b_, 0, 0, nj + j)),
                  pl.BlockSpec((CONV_W, wc), lambda b_, j: (0, j)),
                  pl.BlockSpec((1, wc), lambda b_, j: (0, j)),
                  pl.BlockSpec((2, nb, LRU_BLOCK, 2 * LRU_BLOCK), lambda b_, j: (0, j, 0, 0)),
                  pl.BlockSpec((2, nb, 1, 2 * LRU_BLOCK), lambda b_, j: (0, j, 0, 0)),
                  pl.BlockSpec((2, wc), lambda b_, j: (0, j))],
        out_specs=pl.BlockSpec(blk, lambda b_, j: (b_, 0, 0, j)),
        out_shape=jax.ShapeDtypeStruct((BATCH, SEG, NSEG, LRU_W), F32),
        scratch_shapes=[pltpu.VMEM((SEG + 3, NSEG, wc), F32),
                        seg_buf, seg_buf, seg_buf, seg_buf,
                        pltpu.VMEM((2, nb, LRU_BLOCK, 2 * LRU_BLOCK), BF16)],
        compiler_params=_cparams(2),
    )(xz, xz, cw, cb, w, b, lam)


def _branch_a_kernel(hf_ref, hb_ref, o_ref, za_ref, hg_ref, ga_ref, w_ref, out_ref, ya_ref):
    @pl.when(pl.program_id(2) == 0)
    def _():
        hg = hg_ref[...]

        def body(r, carry):
            rows = pl.ds(pl.multiple_of(r * NORM_ROWS, NORM_ROWS), NORM_ROWS)
            h = hf_ref[0, rows, :].astype(F32) + hb_ref[0, rows, :].astype(F32)
            parts = []
            for hh in range(HEADS):
                hs = h[:, hh * V_DIM:(hh + 1) * V_DIM]
                ms = jnp.mean(hs * hs, axis=-1, keepdims=True)
                parts.append(hs * lax.rsqrt(ms + NORM_EPS))
            hn = jnp.concatenate(parts, axis=1) * hg
            o = o_ref[0, rows, :].astype(F32)
            z = za_ref[0, rows, :].astype(F32)
            ya_ref[rows, :] = (hn * _sigmoid(o) * (z * _sigmoid(z))).astype(BF16)
            return carry

        lax.fori_loop(0, ya_ref.shape[0] // NORM_ROWS, body, 0)

    a = jnp.dot(ya_ref[...], w_ref[0], preferred_element_type=F32)
    out_ref[0] = (_sigmoid(ga_ref[0].astype(F32)) * a).astype(out_ref.dtype)


def _branch_a(hf, hb, head, gg, hg, w, layer, tm, tn):
    wide = (1, tm, M_WIDTH)
    return pl.pallas_call(
        _branch_a_kernel,
        grid=(BATCH, SEQ // tm, D_MODEL // tn),
        in_specs=[pl.BlockSpec(wide, lambda b, i, j: (b, i, 0)),
                  pl.BlockSpec(wide, lambda b, i, j: (b, i, 0)),
                  pl.BlockSpec(wide, lambda b, i, j: (b, i, HEAD_O // M_WIDTH)),
                  pl.BlockSpec(wide, lambda b, i, j: (b, i, HEAD_ZA // M_WIDTH)),
                  pl.BlockSpec((1, M_WIDTH), lambda b, i, j: (0, 0)),
                  pl.BlockSpec((1, tm, tn), lambda b, i, j: (b, i, j)),
                  pl.BlockSpec((1, M_WIDTH, tn), lambda b, i, j: (layer, 0, j))],
        out_specs=pl.BlockSpec((1, tm, tn), lambda b, i, j: (b, i, j)),
        out_shape=jax.ShapeDtypeStruct((BATCH, SEQ, D_MODEL), BF16),
        scratch_shapes=[pltpu.VMEM((tm, M_WIDTH), BF16)],
        compiler_params=_cparams(3),
    )(hf, hb, head, head, hg, gg, w)


def _branch_b_kernel(yb_ref, gb_ref, pa_ref, w_ref, out_ref, y16_ref):
    rows = NSEG * SEG_TILE
    tn = out_ref.shape[-1]

    @pl.when(pl.program_id(2) == 0)
    def _():
        step = 512
        for c in range(LRU_W // step):
            y = jnp.swapaxes(yb_ref[0, :, :, c * step:(c + 1) * step], 0, 1)
            y16_ref[:, c * step:(c + 1) * step] = y.reshape(rows, step).astype(BF16)

    b = jnp.dot(y16_ref[...], w_ref[0], preferred_element_type=F32)
    pa = pa_ref[0].reshape(rows, tn).astype(F32)
    gb = gb_ref[0].reshape(rows, tn).astype(F32)
    out_ref[0] = (pa + _sigmoid(gb) * b).astype(out_ref.dtype).reshape(NSEG, SEG_TILE, tn)


def _branch_b(yb, gg, part_a, w, layer, tn):
    seg4 = lambda arr: arr.reshape(BATCH, NSEG, SEG, arr.shape[-1])
    tile = (1, NSEG, SEG_TILE, tn)
    merged = pl.pallas_call(
        _branch_b_kernel,
        grid=(BATCH, SEG // SEG_TILE, D_MODEL // tn),
        in_specs=[pl.BlockSpec((1, SEG_TILE, NSEG, LRU_W), lambda b, i, j: (b, i, 0, 0)),
                  pl.BlockSpec(tile, lambda b, i, j: (b, 0, i, D_MODEL // tn + j)),
                  pl.BlockSpec(tile, lambda b, i, j: (b, 0, i, j)),
                  pl.BlockSpec((1, LRU_W, tn), lambda b, i, j: (layer, 0, j))],
        out_specs=pl.BlockSpec(tile, lambda b, i, j: (b, 0, i, j)),
        out_shape=jax.ShapeDtypeStruct((BATCH, NSEG, SEG, D_MODEL), BF16),
        scratch_shapes=[pltpu.VMEM((NSEG * SEG_TILE, LRU_W), BF16)],
        compiler_params=_cparams(3),
    )(yb, seg4(gg), seg4(part_a), w)
    return merged.reshape(BATCH, SEQ, D_MODEL)


def _out_kernel(m_ref, x_ref, w_ref, fg_ref, o_ref, *, final_norm):
    y = x_ref[0] + jnp.dot(m_ref[0], w_ref[0], preferred_element_type=F32)
    if final_norm:
        ms = jnp.mean(y * y, axis=-1, keepdims=True)
        y = y * lax.rsqrt(ms + NORM_EPS) * fg_ref[...]
    o_ref[0] = y


def _out_proj(merged, x, w, layer, fg, final_norm, tm):
    row = lambda b, i: (b, i, 0)
    return pl.pallas_call(
        functools.partial(_out_kernel, final_norm=final_norm),
        grid=(BATCH, SEQ // tm),
        in_specs=[pl.BlockSpec((1, tm, D_MODEL), row),
                  pl.BlockSpec((1, tm, D_MODEL), row),
                  pl.BlockSpec((1, D_MODEL, D_MODEL), lambda b, i: (layer, 0, 0)),
                  pl.BlockSpec((1, D_MODEL), lambda b, i: (0, 0))],
        out_specs=pl.BlockSpec((1, tm, D_MODEL), row),
        out_shape=jax.ShapeDtypeStruct((BATCH, SEQ, D_MODEL), F32),
        compiler_params=_cparams(2),
    )(merged, x, w, fg)


def _gate_layouts(wg, bg):
    idx_a = np.zeros((HEADS, 8), np.int32)
    msk_a = np.zeros((HEADS, 8), np.float32)
    idx_b = np.zeros((HEADS, 8), np.int32)
    msk_b = np.zeros((HEADS, 8), np.float32)
    for h in range(HEADS):
        idx_a[h, 0], msk_a[h, 0] = h, 1.0
        idx_a[h, 2], msk_a[h, 2] = HEADS + h, 1.0
        idx_b[h, 0:2], msk_b[h, 0:2] = 2 * HEADS + h, 1.0
        idx_b[h, 2:4], msk_b[h, 2:4] = 3 * HEADS + h, 1.0
    idx_a, msk_a = idx_a.reshape(-1), msk_a.reshape(-1, 1)
    idx_b, msk_b = idx_b.reshape(-1), msk_b.reshape(-1, 1)
    wgt = jnp.transpose(wg, (0, 2, 1))
    wa = (wgt[:, idx_a] * msk_a).astype(BF16)
    wb = (wgt[:, idx_b] * msk_b).astype(BF16)
    ba = bg[:, idx_a][:, :, None] * msk_a
    bb = bg[:, idx_b][:, :, None] * msk_b
    idx_c = np.zeros((128,), np.int32)
    msk_c = np.zeros((1, 128), np.float32)
    idx_c[:2 * HEADS] = 2 * HEADS + np.arange(2 * HEADS)
    msk_c[0, :2 * HEADS] = 1.0
    wc = (wg[:, :, idx_c] * msk_c).astype(BF16)
    bc = bg[:, idx_c][:, None, :] * msk_c
    return wa, wb, ba, bb, wc, bc


def kernel(x, norm_g, w_in, b_if, head_g, conv_w, conv_b, w_rg, b_rg, lru_lambda,
           w_branch_a, w_branch_b, w_out, final_g):
    wkt = jnp.transpose(w_in[:, :, _K0:_V0], (0, 2, 1)).astype(BF16)
    wa, wb, ba, bb, wc, bc = _gate_layouts(w_in[:, :, _G0:_TAIL0], b_if)
    wk = jnp.concatenate([wkt, wa, wb], axis=1)
    n_blk = LRU_W // LRU_BLOCK
    w_gate = jnp.transpose(w_rg, (0, 1, 3, 4, 2, 5)).reshape(
        DEPTH, 2, n_blk, LRU_BLOCK, 2 * LRU_BLOCK).astype(BF16)
    b_gate = jnp.transpose(b_rg.reshape(DEPTH, 2, 2, n_blk, LRU_BLOCK), (0, 1, 3, 2, 4)).reshape(
        DEPTH, 2, n_blk, 1, 2 * LRU_BLOCK)
    w_a16 = w_branch_a.astype(BF16)
    w_b16 = w_branch_b.astype(BF16)
    w_o16 = w_out.astype(BF16)

    head_col = lambda j: jnp.where(j < 4, j + 4, jnp.where(j < 6, j - 2, 0))
    xz_cols = 2 * LRU_W // MM_TN

    h = x
    for l in range(DEPTH):
        hn, kt, grow, gcol = _kgate(h, norm_g[l][None, :], wk, l, ba[l], bb[l], wc[l], bc[l], tm=512)
        head = _proj(hn, w_in, l, 0, head_col, N_HEAD // MM_TN, BF16)
        gg = _proj(hn, w_in, l, _TAIL0, lambda j: xz_cols + j, 2 * D_MODEL // MM_TN, BF16)
        xz = _proj(hn, w_in, l, _TAIL0, lambda j: j, xz_cols, F32, segmented=True)

        hf, hb = _mlstm(head, kt, grow, gcol)
        yb = _lru(xz, conv_w[l], conv_b[l][None, :], w_gate[l], b_gate[l], lru_lambda[l])

        part_a = _branch_a(hf, hb, head, gg, head_g[l][None, :], w_a16, l, tm=512, tn=1024)
        merged = _branch_b(yb, gg, part_a, w_b16, l, tn=1024)
        h = _out_proj(merged, h, w_o16, l, final_g[None, :], l == DEPTH - 1, tm=512)
    return h
```

```python
import functools

import jax
import jax.numpy as jnp
import numpy as np
from jax import lax
from jax.experimental import pallas as pl
from jax.experimental.pallas import tpu as pltpu

F32 = jnp.float32
BF16 = jnp.bfloat16

D_MODEL = 2048
BATCH = 4
SEQ = 4096
DEPTH = 2
HEADS = 4
QK_DIM = 256
V_DIM = 512
QK_WIDTH = HEADS * QK_DIM
M_WIDTH = HEADS * V_DIM
N_GATE = 4 * HEADS
LRU_W = D_MODEL
LRU_BLOCK = 128
LRU_C = 8.0
CONV_W = 4
NORM_EPS = 1e-6
QK_SCALE = QK_DIM ** -0.5

_K0, _V0 = 1024, 2048
_G0 = 8192
_TAIL0 = _G0 + N_GATE

HEAD_O, HEAD_ZA, HEAD_V, HEAD_Q = 0, 2048, 4096, 6144
N_HEAD = 7168

CHUNK = 256
N_CHUNK = SEQ // CHUNK
V_AUG = V_DIM + 128

NSEG = 8
SEG = SEQ // NSEG
LRU_WC = 256
LRU_TC = 16

SEG_TILE = 64
KG_ROWS = QK_WIDTH + 16 * HEADS

NORM_ROWS = 128
MM_TN = 1024
PROJ_TM = 1024
VMEM_LIMIT = 56 * 1024 * 1024


def _cparams(n_axes):
    return pltpu.CompilerParams(dimension_semantics=("arbitrary",) * n_axes,
                                vmem_limit_bytes=VMEM_LIMIT)


def _sigmoid(x):
    return 0.5 * jnp.tanh(0.5 * x) + 0.5


def _log_sigmoid(x):
    return jnp.minimum(x, 0.0) - jnp.log1p(jnp.exp(-jnp.abs(x)))


def _softplus(x):
    return jnp.maximum(x, 0.0) + jnp.log1p(jnp.exp(-jnp.abs(x)))


def _chunk_cumsum(x, axis, reverse):
    n = x.shape[axis]
    idx = lax.broadcasted_iota(jnp.int32, x.shape, axis) % CHUNK
    d = 1
    while d < CHUNK:
        if reverse:
            x = x + jnp.where(idx < CHUNK - d, pltpu.roll(x, n - d, axis), 0.0)
        else:
            x = x + jnp.where(idx >= d, pltpu.roll(x, d, axis), 0.0)
        d *= 2
    return x


def _kgate_kernel(x_ref, g_ref, wk_ref, wg_ref, bcol_ref, brow_ref,
                  hn_ref, kt_ref, grow_ref, gcol_ref, wkt_ref, wg16_ref):
    @pl.when((pl.program_id(0) == 0) & (pl.program_id(1) == 0))
    def _():
        step = 256
        for c in range(QK_WIDTH // step):
            wkt_ref[c * step:(c + 1) * step, :] = wk_ref[0, c * step:(c + 1) * step, :].astype(BF16)
        wg16_ref[...] = jnp.zeros_like(wg16_ref)
        wg16_ref[:N_GATE, :] = wg_ref[0].astype(BF16)

    g = g_ref[...]

    def norm_body(r, carry):
        rows = pl.ds(pl.multiple_of(r * NORM_ROWS, NORM_ROWS), NORM_ROWS)
        x = x_ref[0, rows, :]
        ms = jnp.mean(x * x, axis=-1, keepdims=True)
        hn_ref[0, rows, :] = (x * lax.rsqrt(ms + NORM_EPS) * g).astype(BF16)
        return carry

    lax.fori_loop(0, hn_ref.shape[1] // NORM_ROWS, norm_body, 0)
    hn = hn_ref[0]
    nt = (((1,), (1,)), ((), ()))
    kt = lax.dot_general(wkt_ref[...], hn, nt, preferred_element_type=F32)
    kt_ref[...] = (kt * QK_SCALE).astype(BF16)

    wg = wg16_ref[...]
    xc = lax.dot_general(hn, wg, nt, preferred_element_type=F32) + brow_ref[...]
    lfc = _log_sigmoid(xc)
    lane = lax.broadcasted_iota(jnp.int32, lfc.shape, 1)
    gcol_ref[...] = jnp.where(lane < 3 * HEADS, _chunk_cumsum(lfc, 0, False),
                              _chunk_cumsum(lfc, 0, True))

    xr = (lax.dot_general(wg, hn, nt, preferred_element_type=F32)[:N_GATE]
          + bcol_ref[...])
    lf = _log_sigmoid(xr[2 * HEADS:])
    r8 = lax.broadcasted_iota(jnp.int32, lf.shape, 0)
    cum = jnp.where(r8 < HEADS, _chunk_cumsum(lf, 1, False), _chunk_cumsum(lf, 1, True))
    grow_ref[:2 * HEADS, :] = xr[:2 * HEADS] - cum
    grow_ref[2 * HEADS:, :] = cum


def _kgate(x, g, w_t, layer, bcol, brow, tm):
    nt = SEQ // tm
    tok = BATCH * SEQ
    full = lambda shape: pl.BlockSpec(shape, lambda b, i: (0,) * len(shape))
    return pl.pallas_call(
        _kgate_kernel,
        grid=(BATCH, nt),
        in_specs=[pl.BlockSpec((1, tm, D_MODEL), lambda b, i: (b, i, 0)),
                  full((1, D_MODEL)),
                  pl.BlockSpec((1, QK_WIDTH, D_MODEL), lambda b, i: (layer, _K0 // QK_WIDTH, 0)),
                  pl.BlockSpec((1, N_GATE, D_MODEL), lambda b, i: (layer, _G0 // N_GATE, 0)),
                  full((N_GATE, 1)), full((1, 128))],
        out_specs=[pl.BlockSpec((1, tm, D_MODEL), lambda b, i: (b, i, 0)),
                   pl.BlockSpec((QK_WIDTH, tm), lambda b, i: (0, b * nt + i)),
                   pl.BlockSpec((N_GATE, tm), lambda b, i: (0, b * nt + i)),
                   pl.BlockSpec((tm, 128), lambda b, i: (b * nt + i, 0))],
        out_shape=[jax.ShapeDtypeStruct((BATCH, SEQ, D_MODEL), BF16),
                   jax.ShapeDtypeStruct((QK_WIDTH, tok), BF16),
                   jax.ShapeDtypeStruct((N_GATE, tok), F32),
                   jax.ShapeDtypeStruct((tok, 128), F32)],
        scratch_shapes=[pltpu.VMEM((QK_WIDTH, D_MODEL), BF16),
                        pltpu.VMEM((128, D_MODEL), BF16)],
        compiler_params=_cparams(2),
    )(x, g, w_t, w_t, bcol, brow)


W_CAST_ROWS = 128


def _proj_kernel(*refs, shift, segmented):
    if shift:
        a_ref, w_ref, w2_ref, o_ref, w16_ref = refs
    else:
        a_ref, w_ref, o_ref, w16_ref = refs

    @pl.when((pl.program_id(1) == 0) & (pl.program_id(2) == 0))
    def _():
        for r in range(MM_TN // W_CAST_ROWS):
            lo = shift + r * W_CAST_ROWS
            hi = lo + W_CAST_ROWS
            if hi <= MM_TN:
                w = w_ref[0, lo:hi, :]
            else:
                w = jnp.concatenate([w_ref[0, lo:MM_TN, :], w2_ref[0, :hi - MM_TN, :]], axis=0)
            w16_ref[r * W_CAST_ROWS:(r + 1) * W_CAST_ROWS, :] = w.astype(BF16)

    nt = (((1,), (1,)), ((), ()))
    if segmented:
        a = a_ref[0].reshape(NSEG * SEG_TILE, D_MODEL)
        r = lax.dot_general(a, w16_ref[...], nt, preferred_element_type=F32)
        o_ref[0] = jnp.swapaxes(r.reshape(NSEG, SEG_TILE, MM_TN), 0, 1).astype(o_ref.dtype)
    else:
        o_ref[0] = lax.dot_general(a_ref[0], w16_ref[...], nt,
                                   preferred_element_type=F32).astype(o_ref.dtype)


def _proj(hn, w_t, layer, col0, w_col, n_col, out_dtype, segmented=False):
    shift = col0 % MM_TN
    base = col0 // MM_TN
    assert shift % 8 == 0 and MM_TN % max(shift, 1) == 0
    w_specs = [pl.BlockSpec((1, MM_TN, D_MODEL), lambda j, b, i: (layer, base + w_col(j), 0))]
    operands = [w_t]
    if shift:
        per = MM_TN // shift
        w_specs.append(pl.BlockSpec((1, shift, D_MODEL),
                                    lambda j, b, i: (layer, (base + w_col(j) + 1) * per, 0)))
        operands.append(w_t)
    if segmented:
        a = hn.reshape(BATCH, NSEG, SEG, D_MODEL)
        grid = (n_col, BATCH, SEG // SEG_TILE)
        a_spec = pl.BlockSpec((1, NSEG, SEG_TILE, D_MODEL), lambda j, b, i: (b, 0, i, 0))
        out_spec = pl.BlockSpec((1, SEG_TILE, NSEG, MM_TN), lambda j, b, i: (b, i, 0, j))
        out_shape = jax.ShapeDtypeStruct((BATCH, SEG, NSEG, n_col * MM_TN), out_dtype)
    else:
        a = hn
        grid = (n_col, BATCH, SEQ // PROJ_TM)
        a_spec = pl.BlockSpec((1, PROJ_TM, D_MODEL), lambda j, b, i: (b, i, 0))
        out_spec = pl.BlockSpec((1, PROJ_TM, MM_TN), lambda j, b, i: (b, i, j))
        out_shape = jax.ShapeDtypeStruct((BATCH, SEQ, n_col * MM_TN), out_dtype)
    return pl.pallas_call(
        functools.partial(_proj_kernel, shift=shift, segmented=segmented),
        grid=grid,
        in_specs=[a_spec] + w_specs,
        out_specs=out_spec,
        out_shape=out_shape,
        scratch_shapes=[pltpu.VMEM((MM_TN, D_MODEL), BF16)],
        compiler_params=_cparams(3),
    )(a, *operands)


def _mlstm_kernel(qf_ref, ktf_ref, vf_ref, rowf_ref, colf_ref,
                  qb_ref, ktb_ref, vb_ref, rowb_ref, colb_ref,
                  hf_ref, hb_ref, ct_ref, m_ref):
    head = pl.program_id(1)

    @pl.when(pl.program_id(2) == 0)
    def _():
        ct_ref[...] = jnp.zeros_like(ct_ref)
        m_ref[...] = jnp.zeros_like(m_ref)

    L = CHUNK
    jj = lax.broadcasted_iota(jnp.int32, (L, L), 0)
    ss = lax.broadcasted_iota(jnp.int32, (L, L), 1)
    lane = lax.broadcasted_iota(jnp.int32, (L, 128), 1)
    ones_blk = jnp.where(lane == 0, 1.0, 0.0).astype(BF16)

    dirs = ((qf_ref, ktf_ref, vf_ref, rowf_ref, colf_ref, hf_ref),
            (qb_ref, ktb_ref, vb_ref, rowb_ref, colb_ref, hb_ref))
    for d, (q_ref, kt_ref, v_ref, row_ref, col_ref, out_ref) in enumerate(dirs):
        q = q_ref[0]
        kt = kt_ref[...]
        v_aug = jnp.concatenate([v_ref[0], ones_blk], axis=1)
        a_row = row_ref[pl.ds(HEADS * d + head, 1), :]
        b_row = row_ref[pl.ds(2 * HEADS + HEADS * d + head, 1), :]
        b_col = jnp.sum(jnp.where(lane == 2 * HEADS + HEADS * d + head, col_ref[...], 0.0),
                        axis=1, keepdims=True)
        m_prev = m_ref[d, 0:1, 0:1]

        causal = (ss <= jj) if d == 0 else (ss >= jj)
        d_log = jnp.where(causal, b_col + a_row, -jnp.inf)
        inter = b_col + m_prev
        m_row = jnp.maximum(inter, jnp.max(d_log, axis=1, keepdims=True))
        w_intra = jnp.exp(d_log - m_row)
        w_inter = jnp.exp(inter - m_row)

        s = jnp.dot(q, kt, preferred_element_type=F32)
        p = (s * w_intra).astype(BF16)
        ct = ct_ref[d]
        nd = (jnp.dot(p, v_aug, preferred_element_type=F32)
              + w_inter * jnp.dot(q, ct.astype(BF16), preferred_element_type=F32))
        num = nd[:, :V_DIM]
        den = nd[:, V_DIM:V_DIM + 1]
        out_ref[0] = (num / jnp.maximum(jnp.abs(den), jnp.exp(-m_row))).astype(out_ref.dtype)

        g_tot = b_row[:, L - 1:L] if d == 0 else b_row[:, 0:1]
        w_log = g_tot + a_row
        m_new = jnp.maximum(g_tot + m_prev, jnp.max(w_log, axis=1, keepdims=True))
        w_k = jnp.exp(w_log - m_new)
        decay = jnp.exp(g_tot + m_prev - m_new)
        ktw = (kt.astype(F32) * w_k).astype(BF16)
        ct_ref[d] = decay * ct + jnp.dot(ktw, v_aug, preferred_element_type=F32)
        m_ref[d] = jnp.broadcast_to(m_new, (8, 128))


def _mlstm(head, kt, grow, gcol):
    L, nc = CHUNK, N_CHUNK
    rev = lambda c: nc - 1 - c

    def specs(cmap):
        return [
            pl.BlockSpec((1, L, QK_DIM), lambda b, h, c: (b, cmap(c), HEAD_Q // QK_DIM + h)),
            pl.BlockSpec((QK_DIM, L), lambda b, h, c: (h, b * nc + cmap(c))),
            pl.BlockSpec((1, L, V_DIM), lambda b, h, c: (b, cmap(c), HEAD_V // V_DIM + h)),
            pl.BlockSpec((N_GATE, L), lambda b, h, c: (0, b * nc + cmap(c))),
            pl.BlockSpec((L, 128), lambda b, h, c: (b * nc + cmap(c), 0)),
        ]

    fwd = lambda c: c
    out_sds = jax.ShapeDtypeStruct((BATCH, SEQ, M_WIDTH), BF16)
    return pl.pallas_call(
        _mlstm_kernel,
        grid=(BATCH, HEADS, nc),
        in_specs=specs(fwd) + specs(rev),
        out_specs=[pl.BlockSpec((1, L, V_DIM), lambda b, h, c: (b, c, h)),
                   pl.BlockSpec((1, L, V_DIM), lambda b, h, c: (b, rev(c), h))],
        out_shape=[out_sds, out_sds],
        scratch_shapes=[pltpu.VMEM((2, QK_DIM, V_AUG), F32),
                        pltpu.VMEM((2, 8, 128), F32)],
        compiler_params=_cparams(3),
    )(head, kt, head, grow, gcol, head, kt, head, grow, gcol)


def _lru_kernel(x_ref, z_ref, cw_ref, cb_ref, w_ref, b_ref, lam_ref, o_ref,
                xs_ref, hf_ref, pf_ref, hb_ref, pb_ref, wh_ref, a_ref, u_ref):
    wc = x_ref.shape[-1]
    nb = wc // LRU_BLOCK
    tc = LRU_TC
    n_it = SEG // tc
    sub = lax.broadcasted_iota(jnp.int32, (NSEG, wc), 0)

    def copy_body(it, carry):
        t0 = pl.multiple_of(it * tc, tc)
        xs_ref[pl.ds(t0 + 2, tc)] = x_ref[0, pl.ds(t0, tc)]
        return carry

    lax.fori_loop(0, n_it, copy_body, 0)
    for r in range(2):
        prev = pltpu.roll(x_ref[0, SEG - 2 + r], 1, 0)
        xs_ref[r] = jnp.where(sub >= 1, prev, 0.0)
    nxt = pltpu.roll(x_ref[0, 0], NSEG - 1, 0)
    xs_ref[SEG + 2] = jnp.where(sub <= NSEG - 2, nxt, 0.0)

    cw = [cw_ref[t:t + 1, :][None] for t in range(CONV_W)]
    cb = cb_ref[...][None]

    def conv_body(it, carry):
        t0 = pl.multiple_of(it * tc, tc)
        acc = xs_ref[pl.ds(t0, tc)] * cw[0]
        for t in range(1, CONV_W):
            acc = acc + xs_ref[pl.ds(t0 + t, tc)] * cw[t]
        xs_ref[pl.ds(t0, tc)] = cb + acc
        return carry

    lax.fori_loop(0, n_it, conv_body, 0)

    wh_ref[...] = (0.5 * w_ref[...].astype(F32)).astype(BF16)
    bh = 0.5 * b_ref[...]
    hcs = (-0.5 * LRU_C) * _softplus(-lam_ref[...])

    def conv(t0):
        return xs_ref[pl.ds(t0, tc)]

    def gates(xc, d):
        x2 = xc.reshape(tc * NSEG, wc)
        x16 = x2.astype(BF16)
        hx = 0.5 * x2
        a_parts, u_parts = [], []
        for j in range(nb):
            cols = slice(j * LRU_BLOCK, (j + 1) * LRU_BLOCK)
            pre = jnp.dot(x16[:, cols], wh_ref[d, j], preferred_element_type=F32) + bh[d, j]
            t_r = jnp.tanh(pre[:, :LRU_BLOCK])
            t_i = jnp.tanh(pre[:, LRU_BLOCK:])
            h = hcs[d:d + 1, cols]
            log_a = h * t_r + h
            a_parts.append(jnp.exp(log_a))
            th = jnp.tanh(log_a)
            p = -2.0 * th
            q = 1.0 - th
            coef = jnp.where(p > 0.0, p * lax.rsqrt(p * q), 0.0)
            u_parts.append(coef * ((t_i + 1.0) * hx[:, cols]))
        a = jnp.concatenate(a_parts, axis=1).reshape(tc, NSEG, wc)
        u = jnp.concatenate(u_parts, axis=1).reshape(tc, NSEG, wc)
        return a, u

    def chunk_of(it, d):
        return it if d == 0 else n_it - 1 - it

    def gates_to(slot, it, d):
        t0 = pl.multiple_of(chunk_of(it, d) * tc, tc)
        a, u = gates(conv(t0), d)
        a_ref[d, slot] = a
        u_ref[d, slot] = u

    def half_step(it, carry, slot):
        cur = [(a_ref[d, slot], u_ref[d, slot]) for d in range(2)]
        nxt = jnp.minimum(it + 1, n_it - 1)
        for d in range(2):
            gates_to(1 - slot, nxt, d)
        out = []
        for d, (h_ref, p_ref) in enumerate(((hf_ref, pf_ref), (hb_ref, pb_ref))):
            h, p = carry[2 * d], carry[2 * d + 1]
            a, u = cur[d]
            t0 = pl.multiple_of(chunk_of(it, d) * tc, tc)
            hs, ps = [None] * tc, [None] * tc
            order = range(tc) if d == 0 else range(tc - 1, -1, -1)
            for k in order:
                h = a[k] * h + u[k]
                p = a[k] * p
                hs[k], ps[k] = h, p
            h_ref[pl.ds(t0, tc)] = jnp.stack(hs)
            p_ref[pl.ds(t0, tc)] = jnp.stack(ps)
            out += [h, p]
        return tuple(out)

    def scan_body(i2, carry):
        carry = half_step(2 * i2, carry, 0)
        return half_step(2 * i2 + 1, carry, 1)

    for d in range(2):
        gates_to(0, 0, d)
    zeros = jnp.zeros((NSEG, wc), F32)
    ones = jnp.ones((NSEG, wc), F32)
    hf_end, pf_end, hb_end, pb_end = lax.fori_loop(0, n_it // 2, scan_body,
                                                   (zeros, ones, zeros, ones))

    def carry_in(h_end, p_end, d):
        cin = jnp.zeros((NSEG, wc), F32)
        c = jnp.zeros((1, wc), F32)
        order = range(NSEG) if d == 0 else range(NSEG - 1, -1, -1)
        for s in order:
            cin = jnp.where(sub == s, c, cin)
            c = h_end[s:s + 1] + p_end[s:s + 1] * c
        return cin

    cin_f = carry_in(hf_end, pf_end, 0)
    cin_b = carry_in(hb_end, pb_end, 1)

    def out_body(it, carry):
        rows = pl.ds(pl.multiple_of(it * tc, tc), tc)
        z = z_ref[0, rows]
        h = (hf_ref[rows] + pf_ref[rows] * cin_f) + (hb_ref[rows] + pb_ref[rows] * cin_b)
        o_ref[0, rows] = (h * (z * _sigmoid(z))).astype(o_ref.dtype)
        return carry

    lax.fori_loop(0, n_it, out_body, 0)


def _lru(xz, cw, cb, w, b, lam):
    wc = LRU_WC
    nb = wc // LRU_BLOCK
    nj = LRU_W // wc
    blk = (1, SEG, NSEG, wc)
    seg_buf = pltpu.VMEM((SEG, NSEG, wc), F32)
    return pl.pallas_call(
        _lru_kernel,
        grid=(BATCH, nj),
        in_specs=[pl.BlockSpec(blk, lambda b_, j: (b_, 0, 0, j)),
                  pl.BlockSpec(blk, lambda b_, j: (b_, 0, 0, nj + j)),
                  pl.BlockSpec((CONV_W, wc), lambda b_, j: (0, j)),
                  pl.BlockSpec((1, wc), lambda b_, j: (0, j)),
                  pl.BlockSpec((2, nb, LRU_BLOCK, 2 * LRU_BLOCK), lambda b_, j: (0, j, 0, 0)),
                  pl.BlockSpec((2, nb, 1, 2 * LRU_BLOCK), lambda b_, j: (0, j, 0, 0)),
                  pl.BlockSpec((2, wc), lambda b_, j: (0, j))],
        out_specs=pl.BlockSpec(blk, lambda b_, j: (b_, 0, 0, j)),
        out_shape=jax.ShapeDtypeStruct((BATCH, SEG, NSEG, LRU_W), F32),
        scratch_shapes=[pltpu.VMEM((SEG + 3, NSEG, wc), F32),
                        seg_buf, seg_buf, seg_buf, seg_buf,
                        pltpu.VMEM((2, nb, LRU_BLOCK, 2 * LRU_BLOCK), BF16),
                        pltpu.VMEM((2, 2, LRU_TC, NSEG, wc), F32),
                        pltpu.VMEM((2, 2, LRU_TC, NSEG, wc), F32)],
        compiler_params=_cparams(2),
    )(xz, xz, cw, cb, w, b, lam)


def _branch_a_kernel(hf_ref, hb_ref, o_ref, za_ref, hg_ref, ga_ref, w_ref, out_ref, ya_ref):
    @pl.when(pl.program_id(2) == 0)
    def _():
        hg = hg_ref[...]

        def body(r, carry):
            rows = pl.ds(pl.multiple_of(r * NORM_ROWS, NORM_ROWS), NORM_ROWS)
            h = hf_ref[0, rows, :].astype(F32) + hb_ref[0, rows, :].astype(F32)
            parts = []
            for hh in range(HEADS):
                hs = h[:, hh * V_DIM:(hh + 1) * V_DIM]
                ms = jnp.mean(hs * hs, axis=-1, keepdims=True)
                parts.append(hs * lax.rsqrt(ms + NORM_EPS))
            hn = jnp.concatenate(parts, axis=1) * hg
            o = o_ref[0, rows, :].astype(F32)
            z = za_ref[0, rows, :].astype(F32)
            ya_ref[rows, :] = (hn * _sigmoid(o) * (z * _sigmoid(z))).astype(BF16)
            return carry

        lax.fori_loop(0, ya_ref.shape[0] // NORM_ROWS, body, 0)

    a = jnp.dot(ya_ref[...], w_ref[0], preferred_element_type=F32)
    out_ref[0] = (_sigmoid(ga_ref[0].astype(F32)) * a).astype(out_ref.dtype)


def _branch_a(hf, hb, head, gg, hg, w, layer, tm, tn):
    wide = (1, tm, M_WIDTH)
    return pl.pallas_call(
        _branch_a_kernel,
        grid=(BATCH, SEQ // tm, D_MODEL // tn),
        in_specs=[pl.BlockSpec(wide, lambda b, i, j: (b, i, 0)),
                  pl.BlockSpec(wide, lambda b, i, j: (b, i, 0)),
                  pl.BlockSpec(wide, lambda b, i, j: (b, i, HEAD_O // M_WIDTH)),
                  pl.BlockSpec(wide, lambda b, i, j: (b, i, HEAD_ZA // M_WIDTH)),
                  pl.BlockSpec((1, M_WIDTH), lambda b, i, j: (0, 0)),
                  pl.BlockSpec((1, tm, tn), lambda b, i, j: (b, i, j)),
                  pl.BlockSpec((1, M_WIDTH, tn), lambda b, i, j: (layer, 0, j))],
        out_specs=pl.BlockSpec((1, tm, tn), lambda b, i, j: (b, i, j)),
        out_shape=jax.ShapeDtypeStruct((BATCH, SEQ, D_MODEL), BF16),
        scratch_shapes=[pltpu.VMEM((tm, M_WIDTH), BF16)],
        compiler_params=_cparams(3),
    )(hf, hb, head, head, hg, gg, w)


def _branch_b_kernel(yb_ref, gb_ref, pa_ref, w_ref, out_ref, y16_ref):
    rows = NSEG * SEG_TILE
    tn = out_ref.shape[-1]

    @pl.when(pl.program_id(2) == 0)
    def _():
        step = 512
        for c in range(LRU_W // step):
            y = jnp.swapaxes(yb_ref[0, :, :, c * step:(c + 1) * step], 0, 1)
            y16_ref[:, c * step:(c + 1) * step] = y.reshape(rows, step).astype(BF16)

    b = jnp.dot(y16_ref[...], w_ref[0], preferred_element_type=F32)
    pa = pa_ref[0].reshape(rows, tn).astype(F32)
    gb = gb_ref[0].reshape(rows, tn).astype(F32)
    out_ref[0] = (pa + _sigmoid(gb) * b).astype(out_ref.dtype).reshape(NSEG, SEG_TILE, tn)


def _branch_b(yb, gg, part_a, w, layer, tn):
    seg4 = lambda arr: arr.reshape(BATCH, NSEG, SEG, arr.shape[-1])
    tile = (1, NSEG, SEG_TILE, tn)
    merged = pl.pallas_call(
        _branch_b_kernel,
        grid=(BATCH, SEG // SEG_TILE, D_MODEL // tn),
        in_specs=[pl.BlockSpec((1, SEG_TILE, NSEG, LRU_W), lambda b, i, j: (b, i, 0, 0)),
                  pl.BlockSpec(tile, lambda b, i, j: (b, 0, i, D_MODEL // tn + j)),
                  pl.BlockSpec(tile, lambda b, i, j: (b, 0, i, j)),
                  pl.BlockSpec((1, LRU_W, tn), lambda b, i, j: (layer, 0, j))],
        out_specs=pl.BlockSpec(tile, lambda b, i, j: (b, 0, i, j)),
        out_shape=jax.ShapeDtypeStruct((BATCH, NSEG, SEG, D_MODEL), BF16),
        scratch_shapes=[pltpu.VMEM((NSEG * SEG_TILE, LRU_W), BF16)],
        compiler_params=_cparams(3),
    )(yb, seg4(gg), seg4(part_a), w)
    return merged.reshape(BATCH, SEQ, D_MODEL)


def _out_kernel(m_ref, x_ref, w_ref, fg_ref, o_ref, *, final_norm):
    y = x_ref[0] + jnp.dot(m_ref[0], w_ref[0], preferred_element_type=F32)
    if final_norm:
        ms = jnp.mean(y * y, axis=-1, keepdims=True)
        y = y * lax.rsqrt(ms + NORM_EPS) * fg_ref[...]
    o_ref[0] = y


def _out_proj(merged, x, w, layer, fg, final_norm, tm):
    row = lambda b, i: (b, i, 0)
    return pl.pallas_call(
        functools.partial(_out_kernel, final_norm=final_norm),
        grid=(BATCH, SEQ // tm),
        in_specs=[pl.BlockSpec((1, tm, D_MODEL), row),
                  pl.BlockSpec((1, tm, D_MODEL), row),
                  pl.BlockSpec((1, D_MODEL, D_MODEL), lambda b, i: (layer, 0, 0)),
                  pl.BlockSpec((1, D_MODEL), lambda b, i: (0, 0))],
        out_specs=pl.BlockSpec((1, tm, D_MODEL), row),
        out_shape=jax.ShapeDtypeStruct((BATCH, SEQ, D_MODEL), F32),
        compiler_params=_cparams(2),
    )(merged, x, w, fg)


def kernel(x, norm_g, w_in, b_if, head_g, conv_w, conv_b, w_rg, b_rg, lru_lambda,
           w_branch_a, w_branch_b, w_out, final_g):
    w_t = jnp.swapaxes(w_in, 1, 2)
    bg = jnp.pad(b_if, ((0, 0), (0, 128 - N_GATE)))
    n_blk = LRU_W // LRU_BLOCK
    w_gate = jnp.transpose(w_rg, (0, 1, 3, 4, 2, 5)).reshape(
        DEPTH, 2, n_blk, LRU_BLOCK, 2 * LRU_BLOCK).astype(BF16)
    b_gate = jnp.transpose(b_rg.reshape(DEPTH, 2, 2, n_blk, LRU_BLOCK), (0, 1, 3, 2, 4)).reshape(
        DEPTH, 2, n_blk, 1, 2 * LRU_BLOCK)
    w_a16 = w_branch_a.astype(BF16)
    w_b16 = w_branch_b.astype(BF16)
    w_o16 = w_out.astype(BF16)

    head_col = lambda j: jnp.where(j < 4, j + 4, jnp.where(j < 6, j - 2, 0))
    xz_cols = 2 * LRU_W // MM_TN

    h = x
    for l in range(DEPTH):
        hn, kt, grow, gcol = _kgate(h, norm_g[l][None, :], w_t, l, b_if[l][:, None],
                                    bg[l][None, :], tm=512)
        head = _proj(hn, w_t, l, 0, head_col, N_HEAD // MM_TN, BF16)
        gg = _proj(hn, w_t, l, _TAIL0, lambda j: xz_cols + j, 2 * D_MODEL // MM_TN, BF16)
        xz = _proj(hn, w_t, l, _TAIL0, lambda j: j, xz_cols, F32, segmented=True)

        hf, hb = _mlstm(head, kt, grow, gcol)
        yb = _lru(xz, conv_w[l], conv_b[l][None, :], w_gate[l], b_gate[l], lru_lambda[l])

        part_a = _branch_a(hf, hb, head, gg, head_g[l][None, :], w_a16, l, tm=512, tn=1024)
        merged = _branch_b(yb, gg, part_a, w_b16, l, tn=1024)
        h = _out_proj(merged, h, w_o16, l, final_g[None, :], l == DEPTH - 1, tm=512)
    return h
```

```python
import functools

import jax
import jax.numpy as jnp
import numpy as np
from jax import lax
from jax.experimental import pallas as pl
from jax.experimental.pallas import tpu as pltpu

F32 = jnp.float32
BF16 = jnp.bfloat16

D_MODEL = 2048
BATCH = 4
SEQ = 4096
DEPTH = 2
HEADS = 4
QK_DIM = 256
V_DIM = 512
QK_WIDTH = HEADS * QK_DIM
M_WIDTH = HEADS * V_DIM
N_GATE = 4 * HEADS
LRU_W = D_MODEL
LRU_BLOCK = 128
LRU_C = 8.0
CONV_W = 4
NORM_EPS = 1e-6
QK_SCALE = QK_DIM ** -0.5

_K0, _V0 = 1024, 2048
_G0 = 8192
_TAIL0 = _G0 + N_GATE

HEAD_O, HEAD_ZA, HEAD_V, HEAD_Q = 0, 2048, 4096, 6144
N_HEAD = 7168

CHUNK = 256
N_CHUNK = SEQ // CHUNK
V_AUG = V_DIM + 128
MLSTM_HPS = 2

NSEG = 8
SEG = SEQ // NSEG
LRU_WC = 256
LRU_TC = 16

SEG_TILE = 64
KG_ROWS = QK_WIDTH + 16 * HEADS

NORM_ROWS = 128
MM_TN = 1024
PROJ_TM = 1024
VMEM_LIMIT = 56 * 1024 * 1024


def _cparams(n_axes):
    return pltpu.CompilerParams(dimension_semantics=("arbitrary",) * n_axes,
                                vmem_limit_bytes=VMEM_LIMIT)


def _sigmoid(x):
    return 0.5 * jnp.tanh(0.5 * x) + 0.5


def _log_sigmoid(x):
    return jnp.minimum(x, 0.0) - jnp.log1p(jnp.exp(-jnp.abs(x)))


def _softplus(x):
    return jnp.maximum(x, 0.0) + jnp.log1p(jnp.exp(-jnp.abs(x)))


def _chunk_cumsum(x, axis, reverse):
    n = x.shape[axis]
    idx = lax.broadcasted_iota(jnp.int32, x.shape, axis) % CHUNK
    d = 1
    while d < CHUNK:
        if reverse:
            x = x + jnp.where(idx < CHUNK - d, pltpu.roll(x, n - d, axis), 0.0)
        else:
            x = x + jnp.where(idx >= d, pltpu.roll(x, d, axis), 0.0)
        d *= 2
    return x


def _kgate_kernel(x_ref, g_ref, wk_ref, wg_ref, bcol_ref, brow_ref,
                  hn_ref, kt_ref, grow_ref, gcol_ref, wkt_ref, wg16_ref):
    @pl.when((pl.program_id(0) == 0) & (pl.program_id(1) == 0))
    def _():
        step = 256
        for c in range(QK_WIDTH // step):
            wkt_ref[c * step:(c + 1) * step, :] = wk_ref[0, c * step:(c + 1) * step, :].astype(BF16)
        wg16_ref[...] = jnp.zeros_like(wg16_ref)
        wg16_ref[:N_GATE, :] = wg_ref[0].astype(BF16)
        wkt_ref[QK_WIDTH:, :] = wg16_ref[...]

    g = g_ref[...]

    def norm_body(r, carry):
        rows = pl.ds(pl.multiple_of(r * NORM_ROWS, NORM_ROWS), NORM_ROWS)
        x = x_ref[0, rows, :]
        ms = jnp.mean(x * x, axis=-1, keepdims=True)
        hn_ref[0, rows, :] = (x * lax.rsqrt(ms + NORM_EPS) * g).astype(BF16)
        return carry

    lax.fori_loop(0, hn_ref.shape[1] // NORM_ROWS, norm_body, 0)
    hn = hn_ref[0]
    nt = (((1,), (1,)), ((), ()))
    kg = lax.dot_general(wkt_ref[...], hn, nt, preferred_element_type=F32)
    kt_ref[...] = (kg[:QK_WIDTH] * QK_SCALE).astype(BF16)

    wg = wg16_ref[...]
    xc = lax.dot_general(hn, wg, nt, preferred_element_type=F32) + brow_ref[...]
    lfc = _log_sigmoid(xc)
    lane = lax.broadcasted_iota(jnp.int32, lfc.shape, 1)
    gcol_ref[...] = jnp.where(lane < 3 * HEADS, _chunk_cumsum(lfc, 0, False),
                              _chunk_cumsum(lfc, 0, True))

    xr = kg[QK_WIDTH:QK_WIDTH + N_GATE] + bcol_ref[...]
    lf = _log_sigmoid(xr[2 * HEADS:])
    r8 = lax.broadcasted_iota(jnp.int32, lf.shape, 0)
    cum = jnp.where(r8 < HEADS, _chunk_cumsum(lf, 1, False), _chunk_cumsum(lf, 1, True))
    grow_ref[:2 * HEADS, :] = xr[:2 * HEADS] - cum
    grow_ref[2 * HEADS:, :] = cum


def _kgate(x, g, w_t, layer, bcol, brow, tm):
    nt = SEQ // tm
    tok = BATCH * SEQ
    full = lambda shape: pl.BlockSpec(shape, lambda b, i: (0,) * len(shape))
    return pl.pallas_call(
        _kgate_kernel,
        grid=(BATCH, nt),
        in_specs=[pl.BlockSpec((1, tm, D_MODEL), lambda b, i: (b, i, 0)),
                  full((1, D_MODEL)),
                  pl.BlockSpec((1, QK_WIDTH, D_MODEL), lambda b, i: (layer, _K0 // QK_WIDTH, 0)),
                  pl.BlockSpec((1, N_GATE, D_MODEL), lambda b, i: (layer, _G0 // N_GATE, 0)),
                  full((N_GATE, 1)), full((1, 128))],
        out_specs=[pl.BlockSpec((1, tm, D_MODEL), lambda b, i: (b, i, 0)),
                   pl.BlockSpec((QK_WIDTH, tm), lambda b, i: (0, b * nt + i)),
                   pl.BlockSpec((N_GATE, tm), lambda b, i: (0, b * nt + i)),
                   pl.BlockSpec((tm, 128), lambda b, i: (b * nt + i, 0))],
        out_shape=[jax.ShapeDtypeStruct((BATCH, SEQ, D_MODEL), BF16),
                   jax.ShapeDtypeStruct((QK_WIDTH, tok), BF16),
                   jax.ShapeDtypeStruct((N_GATE, tok), F32),
                   jax.ShapeDtypeStruct((tok, 128), F32)],
        scratch_shapes=[pltpu.VMEM((QK_WIDTH + 128, D_MODEL), BF16),
                        pltpu.VMEM((128, D_MODEL), BF16)],
        compiler_params=_cparams(2),
    )(x, g, w_t, w_t, bcol, brow)


W_CAST_ROWS = 128


def _proj_kernel(*refs, shift, segmented):
    if shift:
        a_ref, w_ref, w2_ref, o_ref, w16_ref = refs
    else:
        a_ref, w_ref, o_ref, w16_ref = refs

    @pl.when((pl.program_id(1) == 0) & (pl.program_id(2) == 0))
    def _():
        for r in range(MM_TN // W_CAST_ROWS):
            lo = shift + r * W_CAST_ROWS
            hi = lo + W_CAST_ROWS
            if hi <= MM_TN:
                w = w_ref[0, lo:hi, :]
            else:
                w = jnp.concatenate([w_ref[0, lo:MM_TN, :], w2_ref[0, :hi - MM_TN, :]], axis=0)
            w16_ref[r * W_CAST_ROWS:(r + 1) * W_CAST_ROWS, :] = w.astype(BF16)

    nt = (((1,), (1,)), ((), ()))
    if segmented:
        a = a_ref[0].reshape(NSEG * SEG_TILE, D_MODEL)
        r = lax.dot_general(a, w16_ref[...], nt, preferred_element_type=F32)
        o_ref[0] = jnp.swapaxes(r.reshape(NSEG, SEG_TILE, MM_TN), 0, 1).astype(o_ref.dtype)
    else:
        o_ref[0] = lax.dot_general(a_ref[0], w16_ref[...], nt,
                                   preferred_element_type=F32).astype(o_ref.dtype)


def _proj(hn, w_t, layer, col0, w_col, n_col, out_dtype, segmented=False):
    shift = col0 % MM_TN
    base = col0 // MM_TN
    assert shift % 8 == 0 and MM_TN % max(shift, 1) == 0
    w_specs = [pl.BlockSpec((1, MM_TN, D_MODEL), lambda j, b, i: (layer, base + w_col(j), 0))]
    operands = [w_t]
    if shift:
        per = MM_TN // shift
        w_specs.append(pl.BlockSpec((1, shift, D_MODEL),
                                    lambda j, b, i: (layer, (base + w_col(j) + 1) * per, 0)))
        operands.append(w_t)
    if segmented:
        a = hn.reshape(BATCH, NSEG, SEG, D_MODEL)
        grid = (n_col, BATCH, SEG // SEG_TILE)
        a_spec = pl.BlockSpec((1, NSEG, SEG_TILE, D_MODEL), lambda j, b, i: (b, 0, i, 0))
        out_spec = pl.BlockSpec((1, SEG_TILE, NSEG, MM_TN), lambda j, b, i: (b, i, 0, j))
        out_shape = jax.ShapeDtypeStruct((BATCH, SEG, NSEG, n_col * MM_TN), out_dtype)
    else:
        a = hn
        grid = (n_col, BATCH, SEQ // PROJ_TM)
        a_spec = pl.BlockSpec((1, PROJ_TM, D_MODEL), lambda j, b, i: (b, i, 0))
        out_spec = pl.BlockSpec((1, PROJ_TM, MM_TN), lambda j, b, i: (b, i, j))
        out_shape = jax.ShapeDtypeStruct((BATCH, SEQ, n_col * MM_TN), out_dtype)
    return pl.pallas_call(
        functools.partial(_proj_kernel, shift=shift, segmented=segmented),
        grid=grid,
        in_specs=[a_spec] + w_specs,
        out_specs=out_spec,
        out_shape=out_shape,
        scratch_shapes=[pltpu.VMEM((MM_TN, D_MODEL), BF16)],
        compiler_params=_cparams(3),
    )(a, *operands)


def _mlstm_kernel(qf_ref, ktf_ref, vf_ref, rowf_ref, colf_ref,
                  qb_ref, ktb_ref, vb_ref, rowb_ref, colb_ref,
                  hf_ref, hb_ref, ct_ref, m_ref):
    @pl.when(pl.program_id(2) == 0)
    def _():
        ct_ref[...] = jnp.zeros_like(ct_ref)
        m_ref[...] = jnp.zeros_like(m_ref)

    L = CHUNK
    jj = lax.broadcasted_iota(jnp.int32, (L, L), 0)
    ss = lax.broadcasted_iota(jnp.int32, (L, L), 1)
    lane = lax.broadcasted_iota(jnp.int32, (L, 128), 1)
    ones_blk = jnp.where(lane == 0, 1.0, 0.0).astype(BF16)

    dirs = ((qf_ref, ktf_ref, vf_ref, rowf_ref, colf_ref, hf_ref),
            (qb_ref, ktb_ref, vb_ref, rowb_ref, colb_ref, hb_ref))
    chains = [(hh, d) for hh in range(MLSTM_HPS) for d in range(2)]
    for hh, d in chains:
        q_ref, kt_ref, v_ref, row_ref, col_ref, out_ref = dirs[d]
        head = pl.program_id(1) * MLSTM_HPS + hh
        st = d * MLSTM_HPS + hh
        q = q_ref[0, :, hh * QK_DIM:(hh + 1) * QK_DIM]
        kt = kt_ref[hh * QK_DIM:(hh + 1) * QK_DIM, :]
        v_aug = jnp.concatenate([v_ref[0, :, hh * V_DIM:(hh + 1) * V_DIM], ones_blk],
                                axis=1)
        a_row = row_ref[pl.ds(HEADS * d + head, 1), :]
        b_row = row_ref[pl.ds(2 * HEADS + HEADS * d + head, 1), :]
        b_col = jnp.sum(jnp.where(lane == 2 * HEADS + HEADS * d + head, col_ref[...], 0.0),
                        axis=1, keepdims=True)
        m_prev = m_ref[st, 0:1, 0:1]

        causal = (ss <= jj) if d == 0 else (ss >= jj)
        d_log = jnp.where(causal, b_col + a_row, -jnp.inf)
        inter = b_col + m_prev
        m_row = jnp.maximum(inter, jnp.max(d_log, axis=1, keepdims=True))
        w_intra = jnp.exp(d_log - m_row)
        w_inter = jnp.exp(inter - m_row)

        s = jnp.dot(q, kt, preferred_element_type=F32)
        p = (s * w_intra).astype(BF16)
        ct = ct_ref[st]
        nd = (jnp.dot(p, v_aug, preferred_element_type=F32)
              + w_inter * jnp.dot(q, ct.astype(BF16), preferred_element_type=F32))
        num = nd[:, :V_DIM]
        den = nd[:, V_DIM:V_DIM + 1]
        out_ref[0, :, hh * V_DIM:(hh + 1) * V_DIM] = (
            num / jnp.maximum(jnp.abs(den), jnp.exp(-m_row))).astype(out_ref.dtype)

        g_tot = b_row[:, L - 1:L] if d == 0 else b_row[:, 0:1]
        w_log = g_tot + a_row
        m_new = jnp.maximum(g_tot + m_prev, jnp.max(w_log, axis=1, keepdims=True))
        w_k = jnp.exp(w_log - m_new)
        decay = jnp.exp(g_tot + m_prev - m_new)
        ktw = (kt.astype(F32) * w_k).astype(BF16)
        ct_ref[st] = decay * ct + jnp.dot(ktw, v_aug, preferred_element_type=F32)
        m_ref[st] = jnp.broadcast_to(m_new, (8, 128))


def _mlstm(head, kt, grow, gcol):
    L, nc, hps = CHUNK, N_CHUNK, MLSTM_HPS
    qw, vw = hps * QK_DIM, hps * V_DIM
    rev = lambda c: nc - 1 - c

    def specs(cmap):
        return [
            pl.BlockSpec((1, L, qw), lambda b, h, c: (b, cmap(c), HEAD_Q // qw + h)),
            pl.BlockSpec((qw, L), lambda b, h, c: (h, b * nc + cmap(c))),
            pl.BlockSpec((1, L, vw), lambda b, h, c: (b, cmap(c), HEAD_V // vw + h)),
            pl.BlockSpec((N_GATE, L), lambda b, h, c: (0, b * nc + cmap(c))),
            pl.BlockSpec((L, 128), lambda b, h, c: (b * nc + cmap(c), 0)),
        ]

    fwd = lambda c: c
    out_sds = jax.ShapeDtypeStruct((BATCH, SEQ, M_WIDTH), BF16)
    return pl.pallas_call(
        _mlstm_kernel,
        grid=(BATCH, HEADS // hps, nc),
        in_specs=specs(fwd) + specs(rev),
        out_specs=[pl.BlockSpec((1, L, vw), lambda b, h, c: (b, c, h)),
                   pl.BlockSpec((1, L, vw), lambda b, h, c: (b, rev(c), h))],
        out_shape=[out_sds, out_sds],
        scratch_shapes=[pltpu.VMEM((2 * hps, QK_DIM, V_AUG), F32),
                        pltpu.VMEM((2 * hps, 8, 128), F32)],
        compiler_params=_cparams(3),
    )(head, kt, head, grow, gcol, head, kt, head, grow, gcol)


def _lru_kernel(x_ref, z_ref, cw_ref, cb_ref, w_ref, b_ref, lam_ref, o_ref,
                xs_ref, hf_ref, pf_ref, hb_ref, pb_ref, wh_ref, a_ref, u_ref):
    wc = x_ref.shape[-1]
    nb = wc // LRU_BLOCK
    tc = LRU_TC
    n_it = SEG // tc
    sub = lax.broadcasted_iota(jnp.int32, (NSEG, wc), 0)

    def copy_body(it, carry):
        t0 = pl.multiple_of(it * tc, tc)
        xs_ref[pl.ds(t0 + 2, tc)] = x_ref[0, pl.ds(t0, tc)]
        return carry

    lax.fori_loop(0, n_it, copy_body, 0)
    for r in range(2):
        prev = pltpu.roll(x_ref[0, SEG - 2 + r], 1, 0)
        xs_ref[r] = jnp.where(sub >= 1, prev, 0.0)
    nxt = pltpu.roll(x_ref[0, 0], NSEG - 1, 0)
    xs_ref[SEG + 2] = jnp.where(sub <= NSEG - 2, nxt, 0.0)

    cw = [cw_ref[t:t + 1, :][None] for t in range(CONV_W)]
    cb = cb_ref[...][None]

    def conv_body(it, carry):
        t0 = pl.multiple_of(it * tc, tc)
        acc = xs_ref[pl.ds(t0, tc)] * cw[0]
        for t in range(1, CONV_W):
            acc = acc + xs_ref[pl.ds(t0 + t, tc)] * cw[t]
        xs_ref[pl.ds(t0, tc)] = cb + acc
        return carry

    lax.fori_loop(0, n_it, conv_body, 0)

    wh_ref[...] = (0.5 * w_ref[...].astype(F32)).astype(BF16)
    bh = 0.5 * b_ref[...]
    hcs = (-0.5 * LRU_C) * _softplus(-lam_ref[...])

    def conv(t0):
        return xs_ref[pl.ds(t0, tc)]

    def gates(xc, d):
        x2 = xc.reshape(tc * NSEG, wc)
        x16 = x2.astype(BF16)
        hx = 0.5 * x2
        a_parts, u_parts = [], []
        for j in range(nb):
            cols = slice(j * LRU_BLOCK, (j + 1) * LRU_BLOCK)
            pre = jnp.dot(x16[:, cols], wh_ref[d, j], preferred_element_type=F32) + bh[d, j]
            t_r = jnp.tanh(pre[:, :LRU_BLOCK])
            t_i = jnp.tanh(pre[:, LRU_BLOCK:])
            h = hcs[d:d + 1, cols]
            log_a = h * t_r + h
            a_parts.append(jnp.exp(log_a))
            th = jnp.tanh(log_a)
            p = -2.0 * th
            q = 1.0 - th
            coef = jnp.where(p > 0.0, p * lax.rsqrt(p * q), 0.0)
            u_parts.append(coef * ((t_i + 1.0) * hx[:, cols]))
        a = jnp.concatenate(a_parts, axis=1).reshape(tc, NSEG, wc)
        u = jnp.concatenate(u_parts, axis=1).reshape(tc, NSEG, wc)
        return a, u

    def chunk_of(it, d):
        return it if d == 0 else n_it - 1 - it

    def gates_to(slot, it, d):
        t0 = pl.multiple_of(chunk_of(it, d) * tc, tc)
        a, u = gates(conv(t0), d)
        a_ref[d, slot] = a
        u_ref[d, slot] = u

    def half_step(it, carry, slot):
        cur = [(a_ref[d, slot], u_ref[d, slot]) for d in range(2)]
        nxt = jnp.minimum(it + 1, n_it - 1)
        for d in range(2):
            gates_to(1 - slot, nxt, d)
        out = []
        for d, (h_ref, p_ref) in enumerate(((hf_ref, pf_ref), (hb_ref, pb_ref))):
            h, p = carry[2 * d], carry[2 * d + 1]
            a, u = cur[d]
            t0 = pl.multiple_of(chunk_of(it, d) * tc, tc)
            hs, ps = [None] * tc, [None] * tc
            order = range(tc) if d == 0 else range(tc - 1, -1, -1)
            for k in order:
                h = a[k] * h + u[k]
                p = a[k] * p
                hs[k], ps[k] = h, p
            h_ref[pl.ds(t0, tc)] = jnp.stack(hs)
            p_ref[pl.ds(t0, tc)] = jnp.stack(ps)
            out += [h, p]
        return tuple(out)

    def scan_body(i2, carry):
        carry = half_step(2 * i2, carry, 0)
        return half_step(2 * i2 + 1, carry, 1)

    for d in range(2):
        gates_to(0, 0, d)
    zeros = jnp.zeros((NSEG, wc), F32)
    ones = jnp.ones((NSEG, wc), F32)
    hf_end, pf_end, hb_end, pb_end = lax.fori_loop(0, n_it // 2, scan_body,
                                                   (zeros, ones, zeros, ones))

    def carry_in(h_end, p_end, d):
        cin = jnp.zeros((NSEG, wc), F32)
        c = jnp.zeros((1, wc), F32)
        order = range(NSEG) if d == 0 else range(NSEG - 1, -1, -1)
        for s in order:
            cin = jnp.where(sub == s, c, cin)
            c = h_end[s:s + 1] + p_end[s:s + 1] * c
        return cin

    cin_f = carry_in(hf_end, pf_end, 0)
    cin_b = carry_in(hb_end, pb_end, 1)

    def out_body(it, carry):
        rows = pl.ds(pl.multiple_of(it * tc, tc), tc)
        z = z_ref[0, rows]
        h = (hf_ref[rows] + pf_ref[rows] * cin_f) + (hb_ref[rows] + pb_ref[rows] * cin_b)
        o_ref[0, rows] = (h * (z * _sigmoid(z))).astype(o_ref.dtype)
        return carry

    lax.fori_loop(0, n_it, out_body, 0)


def _lru(xz, cw, cb, w, b, lam):
    wc = LRU_WC
    nb = wc // LRU_BLOCK
    nj = LRU_W // wc
    blk = (1, SEG, NSEG, wc)
    seg_buf = pltpu.VMEM((SEG, NSEG, wc), F32)
    return pl.pallas_call(
        _lru_kernel,
        grid=(BATCH, nj),
        in_specs=[pl.BlockSpec(blk, lambda b_, j: (b_, 0, 0, j)),
                  pl.BlockSpec(blk, lambda b_, j: (b_, 0, 0, nj + j)),
                  pl.BlockSpec((CONV_W, wc), lambda b_, j: (0, j)),
                  pl.BlockSpec((1, wc), lambda b_, j: (0, j)),
                  pl.BlockSpec((2, nb, LRU_BLOCK, 2 * LRU_BLOCK), lambda b_, j: (0, j, 0, 0)),
                  pl.BlockSpec((2, nb, 1, 2 * LRU_BLOCK), lambda b_, j: (0, j, 0, 0)),
                  pl.BlockSpec((2, wc), lambda b_, j: (0, j))],
        out_specs=pl.BlockSpec(blk, lambda b_, j: (b_, 0, 0, j)),
        out_shape=jax.ShapeDtypeStruct((BATCH, SEG, NSEG, LRU_W), F32),
        scratch_shapes=[pltpu.VMEM((SEG + 3, NSEG, wc), F32),
                        seg_buf, seg_buf, seg_buf, seg_buf,
                        pltpu.VMEM((2, nb, LRU_BLOCK, 2 * LRU_BLOCK), BF16),
                        pltpu.VMEM((2, 2, LRU_TC, NSEG, wc), F32),
                        pltpu.VMEM((2, 2, LRU_TC, NSEG, wc), F32)],
        compiler_params=_cparams(2),
    )(xz, xz, cw, cb, w, b, lam)


def _branch_a_kernel(hf_ref, hb_ref, o_ref, za_ref, hg_ref, ga_ref, w_ref, out_ref, ya_ref):
    @pl.when(pl.program_id(2) == 0)
    def _():
        hg = hg_ref[...]

        def body(r, carry):
            rows = pl.ds(pl.multiple_of(r * NORM_ROWS, NORM_ROWS), NORM_ROWS)
            h = hf_ref[0, rows, :].astype(F32) + hb_ref[0, rows, :].astype(F32)
            parts = []
            for hh in range(HEADS):
                hs = h[:, hh * V_DIM:(hh + 1) * V_DIM]
                ms = jnp.mean(hs * hs, axis=-1, keepdims=True)
                parts.append(hs * lax.rsqrt(ms + NORM_EPS))
            hn = jnp.concatenate(parts, axis=1) * hg
            o = o_ref[0, rows, :].astype(F32)
            z = za_ref[0, rows, :].astype(F32)
            gate = (0.25 * z) * ((1.0 + jnp.tanh(0.5 * o)) * (1.0 + jnp.tanh(0.5 * z)))
            ya_ref[rows, :] = (hn * gate).astype(BF16)
            return carry

        lax.fori_loop(0, ya_ref.shape[0] // NORM_ROWS, body, 0)

    a = jnp.dot(ya_ref[...], w_ref[0], preferred_element_type=F32)
    out_ref[0] = (_sigmoid(ga_ref[0].astype(F32)) * a).astype(out_ref.dtype)


def _branch_a(hf, hb, head, gg, hg, w, layer, tm, tn):
    wide = (1, tm, M_WIDTH)
    return pl.pallas_call(
        _branch_a_kernel,
        grid=(BATCH, SEQ // tm, D_MODEL // tn),
        in_specs=[pl.BlockSpec(wide, lambda b, i, j: (b, i, 0)),
                  pl.BlockSpec(wide, lambda b, i, j: (b, i, 0)),
                  pl.BlockSpec(wide, lambda b, i, j: (b, i, HEAD_O // M_WIDTH)),
                  pl.BlockSpec(wide, lambda b, i, j: (b, i, HEAD_ZA // M_WIDTH)),
                  pl.BlockSpec((1, M_WIDTH), lambda b, i, j: (0, 0)),
                  pl.BlockSpec((1, tm, tn), lambda b, i, j: (b, i, j)),
                  pl.BlockSpec((1, M_WIDTH, tn), lambda b, i, j: (layer, 0, j))],
        out_specs=pl.BlockSpec((1, tm, tn), lambda b, i, j: (b, i, j)),
        out_shape=jax.ShapeDtypeStruct((BATCH, SEQ, D_MODEL), BF16),
        scratch_shapes=[pltpu.VMEM((tm, M_WIDTH), BF16)],
        compiler_params=_cparams(3),
    )(hf, hb, head, head, hg, gg, w)


def _branch_b_kernel(yb_ref, gb_ref, pa_ref, w_ref, out_ref, y16_ref):
    rows = NSEG * SEG_TILE
    tn = out_ref.shape[-1]

    @pl.when(pl.program_id(2) == 0)
    def _():
        step = 512
        for c in range(LRU_W // step):
            y = jnp.swapaxes(yb_ref[0, :, :, c * step:(c + 1) * step], 0, 1)
            y16_ref[:, c * step:(c + 1) * step] = y.reshape(rows, step).astype(BF16)

    b = jnp.dot(y16_ref[...], w_ref[0], preferred_element_type=F32)
    pa = pa_ref[0].reshape(rows, tn).astype(F32)
    gb = gb_ref[0].reshape(rows, tn).astype(F32)
    out_ref[0] = (pa + _sigmoid(gb) * b).astype(out_ref.dtype).reshape(NSEG, SEG_TILE, tn)


def _branch_b(yb, gg, part_a, w, layer, tn):
    seg4 = lambda arr: arr.reshape(BATCH, NSEG, SEG, arr.shape[-1])
    tile = (1, NSEG, SEG_TILE, tn)
    merged = pl.pallas_call(
        _branch_b_kernel,
        grid=(BATCH, SEG // SEG_TILE, D_MODEL // tn),
        in_specs=[pl.BlockSpec((1, SEG_TILE, NSEG, LRU_W), lambda b, i, j: (b, i, 0, 0)),
                  pl.BlockSpec(tile, lambda b, i, j: (b, 0, i, D_MODEL // tn + j)),
                  pl.BlockSpec(tile, lambda b, i, j: (b, 0, i, j)),
                  pl.BlockSpec((1, LRU_W, tn), lambda b, i, j: (layer, 0, j))],
        out_specs=pl.BlockSpec(tile, lambda b, i, j: (b, 0, i, j)),
        out_shape=jax.ShapeDtypeStruct((BATCH, NSEG, SEG, D_MODEL), BF16),
        scratch_shapes=[pltpu.VMEM((NSEG * SEG_TILE, LRU_W), BF16)],
        compiler_params=_cparams(3),
    )(yb, seg4(gg), seg4(part_a), w)
    return merged.reshape(BATCH, SEQ, D_MODEL)


def _out_kernel(m_ref, x_ref, w_ref, fg_ref, o_ref, *, final_norm):
    y = x_ref[0] + jnp.dot(m_ref[0], w_ref[0], preferred_element_type=F32)
    if final_norm:
        ms = jnp.mean(y * y, axis=-1, keepdims=True)
        y = y * lax.rsqrt(ms + NORM_EPS) * fg_ref[...]
    o_ref[0] = y


def _out_proj(merged, x, w, layer, fg, final_norm, tm):
    row = lambda b, i: (b, i, 0)
    return pl.pallas_call(
        functools.partial(_out_kernel, final_norm=final_norm),
        grid=(BATCH, SEQ // tm),
        in_specs=[pl.BlockSpec((1, tm, D_MODEL), row),
                  pl.BlockSpec((1, tm, D_MODEL), row),
                  pl.BlockSpec((1, D_MODEL, D_MODEL), lambda b, i: (layer, 0, 0)),
                  pl.BlockSpec((1, D_MODEL), lambda b, i: (0, 0))],
        out_specs=pl.BlockSpec((1, tm, D_MODEL), row),
        out_shape=jax.ShapeDtypeStruct((BATCH, SEQ, D_MODEL), F32),
        compiler_params=_cparams(2),
    )(merged, x, w, fg)


def kernel(x, norm_g, w_in, b_if, head_g, conv_w, conv_b, w_rg, b_rg, lru_lambda,
           w_branch_a, w_branch_b, w_out, final_g):
    w_t = jnp.swapaxes(w_in, 1, 2)
    bg = jnp.pad(b_if, ((0, 0), (0, 128 - N_GATE)))
    n_blk = LRU_W // LRU_BLOCK
    w_gate = jnp.transpose(w_rg, (0, 1, 3, 4, 2, 5)).reshape(
        DEPTH, 2, n_blk, LRU_BLOCK, 2 * LRU_BLOCK).astype(BF16)
    b_gate = jnp.transpose(b_rg.reshape(DEPTH, 2, 2, n_blk, LRU_BLOCK), (0, 1, 3, 2, 4)).reshape(
        DEPTH, 2, n_blk, 1, 2 * LRU_BLOCK)
    w_a16 = w_branch_a.astype(BF16)
    w_b16 = w_branch_b.astype(BF16)
    w_o16 = w_out.astype(BF16)

    head_col = lambda j: jnp.where(j < 4, j + 4, jnp.where(j < 6, j - 2, 0))
    xz_cols = 2 * LRU_W // MM_TN

    h = x
    for l in range(DEPTH):
        hn, kt, grow, gcol = _kgate(h, norm_g[l][None, :], w_t, l, b_if[l][:, None],
                                    bg[l][None, :], tm=512)
        head = _proj(hn, w_t, l, 0, head_col, N_HEAD // MM_TN, BF16)
        gg = _proj(hn, w_t, l, _TAIL0, lambda j: xz_cols + j, 2 * D_MODEL // MM_TN, BF16)
        xz = _proj(hn, w_t, l, _TAIL0, lambda j: j, xz_cols, F32, segmented=True)

        hf, hb = _mlstm(head, kt, grow, gcol)
        yb = _lru(xz, conv_w[l], conv_b[l][None, :], w_gate[l], b_gate[l], lru_lambda[l])

        part_a = _branch_a(hf, hb, head, gg, head_g[l][None, :], w_a16, l, tm=512, tn=D_MODEL)
        merged = _branch_b(yb, gg, part_a, w_b16, l, tn=D_MODEL)
        h = _out_proj(merged, h, w_o16, l, final_g[None, :], l == DEPTH - 1, tm=512)
    return h
```

```python
import functools

import jax
import jax.numpy as jnp
import numpy as np
from jax import lax
from jax.experimental import pallas as pl
from jax.experimental.pallas import tpu as pltpu

F32 = jnp.float32
BF16 = jnp.bfloat16

D_MODEL = 2048
BATCH = 4
SEQ = 4096
DEPTH = 2
HEADS = 4
QK_DIM = 256
V_DIM = 512
QK_WIDTH = HEADS * QK_DIM
M_WIDTH = HEADS * V_DIM
N_GATE = 4 * HEADS
LRU_W = D_MODEL
LRU_BLOCK = 128
LRU_C = 8.0
CONV_W = 4
NORM_EPS = 1e-6
QK_SCALE = QK_DIM ** -0.5

_K0, _V0 = 1024, 2048
_G0 = 8192
_TAIL0 = _G0 + N_GATE

HEAD_O, HEAD_ZA, HEAD_V, HEAD_Q = 0, 2048, 4096, 6144
N_HEAD = 7168

CHUNK = 256
N_CHUNK = SEQ // CHUNK
V_AUG = V_DIM + 128
MLSTM_HPS = 2

NSEG = 8
SEG = SEQ // NSEG
LRU_WC = 256
LRU_TC = 32

SEG_TILE = 64
KG_ROWS = QK_WIDTH + 16 * HEADS

NORM_ROWS = 128
MM_TN = 1024
PROJ_TM = 1024
VMEM_LIMIT = 56 * 1024 * 1024


def _cparams(n_axes):
    return pltpu.CompilerParams(dimension_semantics=("arbitrary",) * n_axes,
                                vmem_limit_bytes=VMEM_LIMIT)


def _sigmoid(x):
    return 0.5 * jnp.tanh(0.5 * x) + 0.5


def _log_sigmoid(x):
    return jnp.minimum(x, 0.0) - jnp.log1p(jnp.exp(-jnp.abs(x)))


def _softplus(x):
    return jnp.maximum(x, 0.0) + jnp.log1p(jnp.exp(-jnp.abs(x)))


def _chunk_cumsum(x, axis, reverse):
    n = x.shape[axis]
    idx = lax.broadcasted_iota(jnp.int32, x.shape, axis) % CHUNK
    d = 1
    while d < CHUNK:
        if reverse:
            x = x + jnp.where(idx < CHUNK - d, pltpu.roll(x, n - d, axis), 0.0)
        else:
            x = x + jnp.where(idx >= d, pltpu.roll(x, d, axis), 0.0)
        d *= 2
    return x


def _kgate_kernel(x_ref, g_ref, wk_ref, wg_ref, bcol_ref, brow_ref,
                  hn_ref, kt_ref, grow_ref, gcol_ref, wkt_ref, wg16_ref):
    @pl.when((pl.program_id(0) == 0) & (pl.program_id(1) == 0))
    def _():
        step = 256
        for c in range(QK_WIDTH // step):
            wkt_ref[c * step:(c + 1) * step, :] = wk_ref[0, c * step:(c + 1) * step, :].astype(BF16)
        wg16_ref[...] = jnp.zeros_like(wg16_ref)
        wg16_ref[:N_GATE, :] = wg_ref[0].astype(BF16)
        wkt_ref[QK_WIDTH:, :] = wg16_ref[...]

    g = g_ref[...]

    def norm_body(r, carry):
        rows = pl.ds(pl.multiple_of(r * NORM_ROWS, NORM_ROWS), NORM_ROWS)
        x = x_ref[0, rows, :]
        ms = jnp.mean(x * x, axis=-1, keepdims=True)
        hn_ref[0, rows, :] = (x * lax.rsqrt(ms + NORM_EPS) * g).astype(BF16)
        return carry

    lax.fori_loop(0, hn_ref.shape[1] // NORM_ROWS, norm_body, 0)
    hn = hn_ref[0]
    nt = (((1,), (1,)), ((), ()))
    kg = lax.dot_general(wkt_ref[...], hn, nt, preferred_element_type=F32)
    kt_ref[...] = (kg[:QK_WIDTH] * QK_SCALE).astype(BF16)

    wg = wg16_ref[...]
    xc = lax.dot_general(hn, wg, nt, preferred_element_type=F32) + brow_ref[...]
    lfc = _log_sigmoid(xc)
    lane = lax.broadcasted_iota(jnp.int32, lfc.shape, 1)
    gcol_ref[...] = jnp.where(lane < 3 * HEADS, _chunk_cumsum(lfc, 0, False),
                              _chunk_cumsum(lfc, 0, True))

    xr = kg[QK_WIDTH:QK_WIDTH + N_GATE] + bcol_ref[...]
    lf = _log_sigmoid(xr[2 * HEADS:])
    r8 = lax.broadcasted_iota(jnp.int32, lf.shape, 0)
    cum = jnp.where(r8 < HEADS, _chunk_cumsum(lf, 1, False), _chunk_cumsum(lf, 1, True))
    grow_ref[:2 * HEADS, :] = xr[:2 * HEADS] - cum
    grow_ref[2 * HEADS:, :] = cum


def _kgate(x, g, w_t, layer, bcol, brow, tm):
    nt = SEQ // tm
    tok = BATCH * SEQ
    full = lambda shape: pl.BlockSpec(shape, lambda b, i: (0,) * len(shape))
    return pl.pallas_call(
        _kgate_kernel,
        grid=(BATCH, nt),
        in_specs=[pl.BlockSpec((1, tm, D_MODEL), lambda b, i: (b, i, 0)),
                  full((1, D_MODEL)),
                  pl.BlockSpec((1, QK_WIDTH, D_MODEL), lambda b, i: (layer, _K0 // QK_WIDTH, 0)),
                  pl.BlockSpec((1, N_GATE, D_MODEL), lambda b, i: (layer, _G0 // N_GATE, 0)),
                  full((N_GATE, 1)), full((1, 128))],
        out_specs=[pl.BlockSpec((1, tm, D_MODEL), lambda b, i: (b, i, 0)),
                   pl.BlockSpec((QK_WIDTH, tm), lambda b, i: (0, b * nt + i)),
                   pl.BlockSpec((N_GATE, tm), lambda b, i: (0, b * nt + i)),
                   pl.BlockSpec((tm, 128), lambda b, i: (b * nt + i, 0))],
        out_shape=[jax.ShapeDtypeStruct((BATCH, SEQ, D_MODEL), BF16),
                   jax.ShapeDtypeStruct((QK_WIDTH, tok), BF16),
                   jax.ShapeDtypeStruct((N_GATE, tok), F32),
                   jax.ShapeDtypeStruct((tok, 128), F32)],
        scratch_shapes=[pltpu.VMEM((QK_WIDTH + 128, D_MODEL), BF16),
                        pltpu.VMEM((128, D_MODEL), BF16)],
        compiler_params=_cparams(2),
    )(x, g, w_t, w_t, bcol, brow)


W_CAST_ROWS = 128


def _proj_kernel(*refs, shift, segmented):
    if shift:
        a_ref, w_ref, w2_ref, o_ref, w16_ref = refs
    else:
        a_ref, w_ref, o_ref, w16_ref = refs

    @pl.when((pl.program_id(1) == 0) & (pl.program_id(2) == 0))
    def _():
        for r in range(MM_TN // W_CAST_ROWS):
            lo = shift + r * W_CAST_ROWS
            hi = lo + W_CAST_ROWS
            if hi <= MM_TN:
                w = w_ref[0, lo:hi, :]
            else:
                w = jnp.concatenate([w_ref[0, lo:MM_TN, :], w2_ref[0, :hi - MM_TN, :]], axis=0)
            w16_ref[r * W_CAST_ROWS:(r + 1) * W_CAST_ROWS, :] = w.astype(BF16)

    nt = (((1,), (1,)), ((), ()))
    if segmented:
        a = a_ref[0].reshape(NSEG * SEG_TILE, D_MODEL)
        r = lax.dot_general(a, w16_ref[...], nt, preferred_element_type=F32)
        o_ref[0] = jnp.swapaxes(r.reshape(NSEG, SEG_TILE, MM_TN), 0, 1).astype(o_ref.dtype)
    else:
        o_ref[0] = lax.dot_general(a_ref[0], w16_ref[...], nt,
                                   preferred_element_type=F32).astype(o_ref.dtype)


def _proj(hn, w_t, layer, col0, w_col, n_col, out_dtype, segmented=False):
    shift = col0 % MM_TN
    base = col0 // MM_TN
    assert shift % 8 == 0 and MM_TN % max(shift, 1) == 0
    w_specs = [pl.BlockSpec((1, MM_TN, D_MODEL), lambda j, b, i: (layer, base + w_col(j), 0))]
    operands = [w_t]
    if shift:
        per = MM_TN // shift
        w_specs.append(pl.BlockSpec((1, shift, D_MODEL),
                                    lambda j, b, i: (layer, (base + w_col(j) + 1) * per, 0)))
        operands.append(w_t)
    if segmented:
        a = hn.reshape(BATCH, NSEG, SEG, D_MODEL)
        grid = (n_col, BATCH, SEG // SEG_TILE)
        a_spec = pl.BlockSpec((1, NSEG, SEG_TILE, D_MODEL), lambda j, b, i: (b, 0, i, 0))
        out_spec = pl.BlockSpec((1, SEG_TILE, NSEG, MM_TN), lambda j, b, i: (b, i, 0, j))
        out_shape = jax.ShapeDtypeStruct((BATCH, SEG, NSEG, n_col * MM_TN), out_dtype)
    else:
        a = hn
        grid = (n_col, BATCH, SEQ // PROJ_TM)
        a_spec = pl.BlockSpec((1, PROJ_TM, D_MODEL), lambda j, b, i: (b, i, 0))
        out_spec = pl.BlockSpec((1, PROJ_TM, MM_TN), lambda j, b, i: (b, i, j))
        out_shape = jax.ShapeDtypeStruct((BATCH, SEQ, n_col * MM_TN), out_dtype)
    return pl.pallas_call(
        functools.partial(_proj_kernel, shift=shift, segmented=segmented),
        grid=grid,
        in_specs=[a_spec] + w_specs,
        out_specs=out_spec,
        out_shape=out_shape,
        scratch_shapes=[pltpu.VMEM((MM_TN, D_MODEL), BF16)],
        compiler_params=_cparams(3),
    )(a, *operands)


def _mlstm_kernel(qf_ref, ktf_ref, vf_ref, rowf_ref, colf_ref,
                  qb_ref, ktb_ref, vb_ref, rowb_ref, colb_ref,
                  hf_ref, hb_ref, ct_ref, m_ref):
    @pl.when(pl.program_id(2) == 0)
    def _():
        ct_ref[...] = jnp.zeros_like(ct_ref)
        m_ref[...] = jnp.zeros_like(m_ref)

    L = CHUNK
    jj = lax.broadcasted_iota(jnp.int32, (L, L), 0)
    ss = lax.broadcasted_iota(jnp.int32, (L, L), 1)
    lane = lax.broadcasted_iota(jnp.int32, (L, 128), 1)
    ones_blk = jnp.where(lane == 0, 1.0, 0.0).astype(BF16)

    dirs = ((qf_ref, ktf_ref, vf_ref, rowf_ref, colf_ref, hf_ref),
            (qb_ref, ktb_ref, vb_ref, rowb_ref, colb_ref, hb_ref))
    chains = [(hh, d) for hh in range(MLSTM_HPS) for d in range(2)]
    for hh, d in chains:
        q_ref, kt_ref, v_ref, row_ref, col_ref, out_ref = dirs[d]
        head = pl.program_id(1) * MLSTM_HPS + hh
        st = d * MLSTM_HPS + hh
        q = q_ref[0, :, hh * QK_DIM:(hh + 1) * QK_DIM]
        kt = kt_ref[hh * QK_DIM:(hh + 1) * QK_DIM, :]
        v_aug = jnp.concatenate([v_ref[0, :, hh * V_DIM:(hh + 1) * V_DIM], ones_blk],
                                axis=1)
        a_row = row_ref[pl.ds(HEADS * d + head, 1), :]
        b_row = row_ref[pl.ds(2 * HEADS + HEADS * d + head, 1), :]
        b_col = jnp.sum(jnp.where(lane == 2 * HEADS + HEADS * d + head, col_ref[...], 0.0),
                        axis=1, keepdims=True)
        m_prev = m_ref[st, 0:1, 0:1]

        causal = (ss <= jj) if d == 0 else (ss >= jj)
        d_log = jnp.where(causal, b_col + a_row, -jnp.inf)
        inter = b_col + m_prev
        m_row = jnp.maximum(inter, jnp.max(d_log, axis=1, keepdims=True))
        w_intra = jnp.exp(d_log - m_row)
        w_inter = jnp.exp(inter - m_row)

        s = jnp.dot(q, kt, preferred_element_type=F32)
        p = (s * w_intra).astype(BF16)
        ct = ct_ref[st]
        nd = (jnp.dot(p, v_aug, preferred_element_type=F32)
              + w_inter * jnp.dot(q, ct.astype(BF16), preferred_element_type=F32))
        num = nd[:, :V_DIM]
        den = nd[:, V_DIM:V_DIM + 1]
        out_ref[0, :, hh * V_DIM:(hh + 1) * V_DIM] = (
            num / jnp.maximum(jnp.abs(den), jnp.exp(-m_row))).astype(out_ref.dtype)

        g_tot = b_row[:, L - 1:L] if d == 0 else b_row[:, 0:1]
        w_log = g_tot + a_row
        m_new = jnp.maximum(g_tot + m_prev, jnp.max(w_log, axis=1, keepdims=True))
        w_k = jnp.exp(w_log - m_new)
        decay = jnp.exp(g_tot + m_prev - m_new)
        ktw = (kt.astype(F32) * w_k).astype(BF16)
        ct_ref[st] = decay * ct + jnp.dot(ktw, v_aug, preferred_element_type=F32)
        m_ref[st] = jnp.broadcast_to(m_new, (8, 128))


def _mlstm(head, kt, grow, gcol):
    L, nc, hps = CHUNK, N_CHUNK, MLSTM_HPS
    qw, vw = hps * QK_DIM, hps * V_DIM
    rev = lambda c: nc - 1 - c

    def specs(cmap):
        return [
            pl.BlockSpec((1, L, qw), lambda b, h, c: (b, cmap(c), HEAD_Q // qw + h)),
            pl.BlockSpec((qw, L), lambda b, h, c: (h, b * nc + cmap(c))),
            pl.BlockSpec((1, L, vw), lambda b, h, c: (b, cmap(c), HEAD_V // vw + h)),
            pl.BlockSpec((N_GATE, L), lambda b, h, c: (0, b * nc + cmap(c))),
            pl.BlockSpec((L, 128), lambda b, h, c: (b * nc + cmap(c), 0)),
        ]

    fwd = lambda c: c
    out_sds = jax.ShapeDtypeStruct((BATCH, SEQ, M_WIDTH), BF16)
    return pl.pallas_call(
        _mlstm_kernel,
        grid=(BATCH, HEADS // hps, nc),
        in_specs=specs(fwd) + specs(rev),
        out_specs=[pl.BlockSpec((1, L, vw), lambda b, h, c: (b, c, h)),
                   pl.BlockSpec((1, L, vw), lambda b, h, c: (b, rev(c), h))],
        out_shape=[out_sds, out_sds],
        scratch_shapes=[pltpu.VMEM((2 * hps, QK_DIM, V_AUG), F32),
                        pltpu.VMEM((2 * hps, 8, 128), F32)],
        compiler_params=_cparams(3),
    )(head, kt, head, grow, gcol, head, kt, head, grow, gcol)


def _lru_kernel(x_ref, z_ref, cw_ref, cb_ref, w_ref, b_ref, lam_ref, o_ref,
                xs_ref, hf_ref, pf_ref, hb_ref, pb_ref, wh_ref, a_ref, u_ref):
    wc = x_ref.shape[-1]
    nb = wc // LRU_BLOCK
    tc = LRU_TC
    n_it = SEG // tc
    sub = lax.broadcasted_iota(jnp.int32, (NSEG, wc), 0)

    def copy_body(it, carry):
        t0 = pl.multiple_of(it * tc, tc)
        xs_ref[pl.ds(t0 + 2, tc)] = x_ref[0, pl.ds(t0, tc)]
        return carry

    lax.fori_loop(0, n_it, copy_body, 0)
    for r in range(2):
        prev = pltpu.roll(x_ref[0, SEG - 2 + r], 1, 0)
        xs_ref[r] = jnp.where(sub >= 1, prev, 0.0)
    nxt = pltpu.roll(x_ref[0, 0], NSEG - 1, 0)
    xs_ref[SEG + 2] = jnp.where(sub <= NSEG - 2, nxt, 0.0)

    cw = [cw_ref[t:t + 1, :][None] for t in range(CONV_W)]
    cb = cb_ref[...][None]

    def conv_body(it, carry):
        t0 = pl.multiple_of(it * tc, tc)
        acc = xs_ref[pl.ds(t0, tc)] * cw[0]
        for t in range(1, CONV_W):
            acc = acc + xs_ref[pl.ds(t0 + t, tc)] * cw[t]
        xs_ref[pl.ds(t0, tc)] = cb + acc
        return carry

    lax.fori_loop(0, n_it, conv_body, 0)

    wh_ref[...] = (0.5 * w_ref[...].astype(F32)).astype(BF16)
    bh = 0.5 * b_ref[...]
    hcs = (-0.5 * LRU_C) * _softplus(-lam_ref[...])

    def conv(t0):
        return xs_ref[pl.ds(t0, tc)]

    def gates(xc, d):
        x2 = xc.reshape(tc * NSEG, wc)
        x16 = x2.astype(BF16)
        hx = 0.5 * x2
        a_parts, u_parts = [], []
        for j in range(nb):
            cols = slice(j * LRU_BLOCK, (j + 1) * LRU_BLOCK)
            pre = jnp.dot(x16[:, cols], wh_ref[d, j], preferred_element_type=F32) + bh[d, j]
            t_r = jnp.tanh(pre[:, :LRU_BLOCK])
            t_i = jnp.tanh(pre[:, LRU_BLOCK:])
            h = hcs[d:d + 1, cols]
            log_a = h * t_r + h
            a_parts.append(jnp.exp(log_a))
            th = jnp.tanh(log_a)
            p = -2.0 * th
            q = 1.0 - th
            coef = jnp.where(p > 0.0, p * lax.rsqrt(p * q), 0.0)
            u_parts.append(coef * ((t_i + 1.0) * hx[:, cols]))
        a = jnp.concatenate(a_parts, axis=1).reshape(tc, NSEG, wc)
        u = jnp.concatenate(u_parts, axis=1).reshape(tc, NSEG, wc)
        return a, u

    def chunk_of(it, d):
        return it if d == 0 else n_it - 1 - it

    def gates_to(slot, it, d):
        t0 = pl.multiple_of(chunk_of(it, d) * tc, tc)
        a, u = gates(conv(t0), d)
        a_ref[d, slot] = a
        u_ref[d, slot] = u

    def half_step(it, carry, slot):
        cur = [(a_ref[d, slot], u_ref[d, slot]) for d in range(2)]
        nxt = jnp.minimum(it + 1, n_it - 1)
        for d in range(2):
            gates_to(1 - slot, nxt, d)
        out = []
        for d, (h_ref, p_ref) in enumerate(((hf_ref, pf_ref), (hb_ref, pb_ref))):
            h, p = carry[2 * d], carry[2 * d + 1]
            a, u = cur[d]
            t0 = pl.multiple_of(chunk_of(it, d) * tc, tc)
            hs, ps = [None] * tc, [None] * tc
            order = range(tc) if d == 0 else range(tc - 1, -1, -1)
            for k in order:
                h = a[k] * h + u[k]
                p = a[k] * p
                hs[k], ps[k] = h, p
            h_ref[pl.ds(t0, tc)] = jnp.stack(hs)
            p_ref[pl.ds(t0, tc)] = jnp.stack(ps)
            out += [h, p]
        return tuple(out)

    def scan_body(i2, carry):
        carry = half_step(2 * i2, carry, 0)
        return half_step(2 * i2 + 1, carry, 1)

    for d in range(2):
        gates_to(0, 0, d)
    zeros = jnp.zeros((NSEG, wc), F32)
    ones = jnp.ones((NSEG, wc), F32)
    hf_end, pf_end, hb_end, pb_end = lax.fori_loop(0, n_it // 2, scan_body,
                                                   (zeros, ones, zeros, ones))

    def carry_in(h_end, p_end, d):
        cin = jnp.zeros((NSEG, wc), F32)
        c = jnp.zeros((1, wc), F32)
        order = range(NSEG) if d == 0 else range(NSEG - 1, -1, -1)
        for s in order:
            cin = jnp.where(sub == s, c, cin)
            c = h_end[s:s + 1] + p_end[s:s + 1] * c
        return cin

    cin_f = carry_in(hf_end, pf_end, 0)
    cin_b = carry_in(hb_end, pb_end, 1)

    def out_body(it, carry):
        rows = pl.ds(pl.multiple_of(it * tc, tc), tc)
        z = z_ref[0, rows]
        h = (hf_ref[rows] + pf_ref[rows] * cin_f) + (hb_ref[rows] + pb_ref[rows] * cin_b)
        o_ref[0, rows] = (h * (z * _sigmoid(z))).astype(o_ref.dtype)
        return carry

    lax.fori_loop(0, n_it, out_body, 0)


def _lru(xz, cw, cb, w, b, lam):
    wc = LRU_WC
    nb = wc // LRU_BLOCK
    nj = LRU_W // wc
    blk = (1, SEG, NSEG, wc)
    seg_buf = pltpu.VMEM((SEG, NSEG, wc), F32)
    return pl.pallas_call(
        _lru_kernel,
        grid=(BATCH, nj),
        in_specs=[pl.BlockSpec(blk, lambda b_, j: (b_, 0, 0, j)),
                  pl.BlockSpec(blk, lambda b_, j: (b_, 0, 0, nj + j)),
                  pl.BlockSpec((CONV_W, wc), lambda b_, j: (0, j)),
                  pl.BlockSpec((1, wc), lambda b_, j: (0, j)),
                  pl.BlockSpec((2, nb, LRU_BLOCK, 2 * LRU_BLOCK), lambda b_, j: (0, j, 0, 0)),
                  pl.BlockSpec((2, nb, 1, 2 * LRU_BLOCK), lambda b_, j: (0, j, 0, 0)),
                  pl.BlockSpec((2, wc), lambda b_, j: (0, j))],
        out_specs=pl.BlockSpec(blk, lambda b_, j: (b_, 0, 0, j)),
        out_shape=jax.ShapeDtypeStruct((BATCH, SEG, NSEG, LRU_W), F32),
        scratch_shapes=[pltpu.VMEM((SEG + 3, NSEG, wc), F32),
                        seg_buf, seg_buf, seg_buf, seg_buf,
                        pltpu.VMEM((2, nb, LRU_BLOCK, 2 * LRU_BLOCK), BF16),
                        pltpu.VMEM((2, 2, LRU_TC, NSEG, wc), F32),
                        pltpu.VMEM((2, 2, LRU_TC, NSEG, wc), F32)],
        compiler_params=_cparams(2),
    )(xz, xz, cw, cb, w, b, lam)


def _branch_a_kernel(hf_ref, hb_ref, o_ref, za_ref, hg_ref, ga_ref, w_ref, out_ref, ya_ref):
    tm = ya_ref.shape[0]
    acc = None
    for hh in range(HEADS):
        cols = slice(hh * V_DIM, (hh + 1) * V_DIM)
        hg = hg_ref[:, cols]
        for r in range(tm // NORM_ROWS):
            rows = slice(r * NORM_ROWS, (r + 1) * NORM_ROWS)
            hs = hf_ref[0, rows, cols].astype(F32) + hb_ref[0, rows, cols].astype(F32)
            ms = jnp.mean(hs * hs, axis=-1, keepdims=True)
            hn = hs * lax.rsqrt(ms + NORM_EPS) * hg
            o = o_ref[0, rows, cols].astype(F32)
            z = za_ref[0, rows, cols].astype(F32)
            gate = (0.25 * z) * ((1.0 + jnp.tanh(0.5 * o)) * (1.0 + jnp.tanh(0.5 * z)))
            ya_ref[rows, cols] = (hn * gate).astype(BF16)
        part = jnp.dot(ya_ref[:, cols], w_ref[0, cols, :], preferred_element_type=F32)
        acc = part if acc is None else acc + part
    out_ref[0] = (_sigmoid(ga_ref[0].astype(F32)) * acc).astype(out_ref.dtype)


def _branch_a(hf, hb, head, gg, hg, w, layer, tm):
    wide = (1, tm, M_WIDTH)
    return pl.pallas_call(
        _branch_a_kernel,
        grid=(BATCH, SEQ // tm),
        in_specs=[pl.BlockSpec(wide, lambda b, i: (b, i, 0)),
                  pl.BlockSpec(wide, lambda b, i: (b, i, 0)),
                  pl.BlockSpec(wide, lambda b, i: (b, i, HEAD_O // M_WIDTH)),
                  pl.BlockSpec(wide, lambda b, i: (b, i, HEAD_ZA // M_WIDTH)),
                  pl.BlockSpec((1, M_WIDTH), lambda b, i: (0, 0)),
                  pl.BlockSpec((1, tm, D_MODEL), lambda b, i: (b, i, 0)),
                  pl.BlockSpec((1, M_WIDTH, D_MODEL), lambda b, i: (layer, 0, 0))],
        out_specs=pl.BlockSpec((1, tm, D_MODEL), lambda b, i: (b, i, 0)),
        out_shape=jax.ShapeDtypeStruct((BATCH, SEQ, D_MODEL), BF16),
        scratch_shapes=[pltpu.VMEM((tm, M_WIDTH), BF16)],
        compiler_params=_cparams(2),
    )(hf, hb, head, head, hg, gg, w)


def _branch_b_kernel(yb_ref, gb_ref, pa_ref, w_ref, out_ref, y16_ref):
    rows = NSEG * SEG_TILE
    step = 512
    acc = None
    for c in range(LRU_W // step):
        cols = slice(c * step, (c + 1) * step)
        y = jnp.swapaxes(yb_ref[0, :, :, cols], 0, 1)
        y16_ref[:, cols] = y.reshape(rows, step).astype(BF16)
        part = jnp.dot(y16_ref[:, cols], w_ref[0, cols, :], preferred_element_type=F32)
        acc = part if acc is None else acc + part
    pa = pa_ref[0].reshape(rows, D_MODEL).astype(F32)
    gb = gb_ref[0].reshape(rows, D_MODEL).astype(F32)
    out_ref[0] = (pa + _sigmoid(gb) * acc).astype(out_ref.dtype).reshape(NSEG, SEG_TILE, D_MODEL)


def _branch_b(yb, gg, part_a, w, layer):
    seg4 = lambda arr: arr.reshape(BATCH, NSEG, SEG, arr.shape[-1])
    tile = (1, NSEG, SEG_TILE, D_MODEL)
    merged = pl.pallas_call(
        _branch_b_kernel,
        grid=(BATCH, SEG // SEG_TILE),
        in_specs=[pl.BlockSpec((1, SEG_TILE, NSEG, LRU_W), lambda b, i: (b, i, 0, 0)),
                  pl.BlockSpec(tile, lambda b, i: (b, 0, i, 1)),
                  pl.BlockSpec(tile, lambda b, i: (b, 0, i, 0)),
                  pl.BlockSpec((1, LRU_W, D_MODEL), lambda b, i: (layer, 0, 0))],
        out_specs=pl.BlockSpec(tile, lambda b, i: (b, 0, i, 0)),
        out_shape=jax.ShapeDtypeStruct((BATCH, NSEG, SEG, D_MODEL), BF16),
        scratch_shapes=[pltpu.VMEM((NSEG * SEG_TILE, LRU_W), BF16)],
        compiler_params=_cparams(2),
    )(yb, seg4(gg), seg4(part_a), w)
    return merged.reshape(BATCH, SEQ, D_MODEL)


def _out_kernel(m_ref, x_ref, w_ref, fg_ref, o_ref, *, final_norm):
    y = x_ref[0] + jnp.dot(m_ref[0], w_ref[0], preferred_element_type=F32)
    if final_norm:
        ms = jnp.mean(y * y, axis=-1, keepdims=True)
        y = y * lax.rsqrt(ms + NORM_EPS) * fg_ref[...]
    o_ref[0] = y


def _out_proj(merged, x, w, layer, fg, final_norm, tm):
    row = lambda b, i: (b, i, 0)
    return pl.pallas_call(
        functools.partial(_out_kernel, final_norm=final_norm),
        grid=(BATCH, SEQ // tm),
        in_specs=[pl.BlockSpec((1, tm, D_MODEL), row),
                  pl.BlockSpec((1, tm, D_MODEL), row),
                  pl.BlockSpec((1, D_MODEL, D_MODEL), lambda b, i: (layer, 0, 0)),
                  pl.BlockSpec((1, D_MODEL), lambda b, i: (0, 0))],
        out_specs=pl.BlockSpec((1, tm, D_MODEL), row),
        out_shape=jax.ShapeDtypeStruct((BATCH, SEQ, D_MODEL), F32),
        compiler_params=_cparams(2),
    )(merged, x, w, fg)


def kernel(x, norm_g, w_in, b_if, head_g, conv_w, conv_b, w_rg, b_rg, lru_lambda,
           w_branch_a, w_branch_b, w_out, final_g):
    w_t = jnp.swapaxes(w_in, 1, 2)
    bg = jnp.pad(b_if, ((0, 0), (0, 128 - N_GATE)))
    n_blk = LRU_W // LRU_BLOCK
    w_gate = jnp.transpose(w_rg, (0, 1, 3, 4, 2, 5)).reshape(
        DEPTH, 2, n_blk, LRU_BLOCK, 2 * LRU_BLOCK).astype(BF16)
    b_gate = jnp.transpose(b_rg.reshape(DEPTH, 2, 2, n_blk, LRU_BLOCK), (0, 1, 3, 2, 4)).reshape(
        DEPTH, 2, n_blk, 1, 2 * LRU_BLOCK)
    w_a16 = w_branch_a.astype(BF16)
    w_b16 = w_branch_b.astype(BF16)
    w_o16 = w_out.astype(BF16)

    head_col = lambda j: jnp.where(j < 4, j + 4, jnp.where(j < 6, j - 2, 0))
    xz_cols = 2 * LRU_W // MM_TN

    h = x
    for l in range(DEPTH):
        hn, kt, grow, gcol = _kgate(h, norm_g[l][None, :], w_t, l, b_if[l][:, None],
                                    bg[l][None, :], tm=512)
        head = _proj(hn, w_t, l, 0, head_col, N_HEAD // MM_TN, BF16)
        gg = _proj(hn, w_t, l, _TAIL0, lambda j: xz_cols + j, 2 * D_MODEL // MM_TN, BF16)
        xz = _proj(hn, w_t, l, _TAIL0, lambda j: j, xz_cols, F32, segmented=True)

        hf, hb = _mlstm(head, kt, grow, gcol)
        yb = _lru(xz, conv_w[l], conv_b[l][None, :], w_gate[l], b_gate[l], lru_lambda[l])

        part_a = _branch_a(hf, hb, head, gg, head_g[l][None, :], w_a16, l, tm=512)
        merged = _branch_b(yb, gg, part_a, w_b16, l)
        h = _out_proj(merged, h, w_o16, l, final_g[None, :], l == DEPTH - 1, tm=512)
    return h
```

```python
import functools

import jax
import jax.numpy as jnp
import numpy as np
from jax import lax
from jax.experimental import pallas as pl
from jax.experimental.pallas import tpu as pltpu

F32 = jnp.float32
BF16 = jnp.bfloat16

D_MODEL = 2048
BATCH = 4
SEQ = 4096
DEPTH = 2
HEADS = 4
QK_DIM = 256
V_DIM = 512
QK_WIDTH = HEADS * QK_DIM
M_WIDTH = HEADS * V_DIM
N_GATE = 4 * HEADS
LRU_W = D_MODEL
LRU_BLOCK = 128
LRU_C = 8.0
CONV_W = 4
NORM_EPS = 1e-6
QK_SCALE = QK_DIM ** -0.5

_K0, _V0 = 1024, 2048
_G0 = 8192
_TAIL0 = _G0 + N_GATE

HEAD_O, HEAD_ZA, HEAD_V, HEAD_Q = 0, 2048, 4096, 6144
N_HEAD = 7168

CHUNK = 256
N_CHUNK = SEQ // CHUNK
V_AUG = V_DIM + 128
MLSTM_HPS = 4

NSEG = 8
SEG = SEQ // NSEG
LRU_WC = 256
LRU_TC = 32

SEG_TILE = 64
PROJ_SEG_TILE = 128
KG_ROWS = QK_WIDTH + 16 * HEADS

NORM_ROWS = 128
MM_TN = 1024
PROJ_TM = 2048
VMEM_LIMIT = 56 * 1024 * 1024


def _cparams(n_axes):
    return pltpu.CompilerParams(dimension_semantics=("arbitrary",) * n_axes,
                                vmem_limit_bytes=VMEM_LIMIT)


def _sigmoid(x):
    return 0.5 * jnp.tanh(0.5 * x) + 0.5


def _log_sigmoid(x):
    return jnp.minimum(x, 0.0) - jnp.log1p(jnp.exp(-jnp.abs(x)))


def _softplus(x):
    return jnp.maximum(x, 0.0) + jnp.log1p(jnp.exp(-jnp.abs(x)))


def _chunk_cumsum(x, axis, reverse):
    n = x.shape[axis]
    idx = lax.broadcasted_iota(jnp.int32, x.shape, axis) % CHUNK
    d = 1
    while d < CHUNK:
        if reverse:
            x = x + jnp.where(idx < CHUNK - d, pltpu.roll(x, n - d, axis), 0.0)
        else:
            x = x + jnp.where(idx >= d, pltpu.roll(x, d, axis), 0.0)
        d *= 2
    return x


def _kgate_kernel(x_ref, g_ref, wk_ref, wg_ref, bcol_ref, brow_ref,
                  hn_ref, kt_ref, grow_ref, gcol_ref, wkt_ref, wg16_ref):
    @pl.when((pl.program_id(0) == 0) & (pl.program_id(1) == 0))
    def _():
        step = 256
        for c in range(QK_WIDTH // step):
            wkt_ref[c * step:(c + 1) * step, :] = wk_ref[0, c * step:(c + 1) * step, :].astype(BF16)
        wg16_ref[...] = jnp.zeros_like(wg16_ref)
        wg16_ref[:N_GATE, :] = wg_ref[0].astype(BF16)
        wkt_ref[QK_WIDTH:, :] = wg16_ref[...]

    g = g_ref[...]

    def norm_body(r, carry):
        rows = pl.ds(pl.multiple_of(r * NORM_ROWS, NORM_ROWS), NORM_ROWS)
        x = x_ref[0, rows, :]
        ms = jnp.mean(x * x, axis=-1, keepdims=True)
        hn_ref[0, rows, :] = (x * lax.rsqrt(ms + NORM_EPS) * g).astype(BF16)
        return carry

    lax.fori_loop(0, hn_ref.shape[1] // NORM_ROWS, norm_body, 0)
    hn = hn_ref[0]
    nt = (((1,), (1,)), ((), ()))
    kg = lax.dot_general(wkt_ref[...], hn, nt, preferred_element_type=F32)
    kt_ref[...] = (kg[:QK_WIDTH] * QK_SCALE).astype(BF16)

    wg = wg16_ref[...]
    xc = lax.dot_general(hn, wg, nt, preferred_element_type=F32) + brow_ref[...]
    lfc = _log_sigmoid(xc)
    lane = lax.broadcasted_iota(jnp.int32, lfc.shape, 1)
    gcol_ref[...] = jnp.where(lane < 3 * HEADS, _chunk_cumsum(lfc, 0, False),
                              _chunk_cumsum(lfc, 0, True))

    xr = kg[QK_WIDTH:QK_WIDTH + N_GATE] + bcol_ref[...]
    lf = _log_sigmoid(xr[2 * HEADS:])
    r8 = lax.broadcasted_iota(jnp.int32, lf.shape, 0)
    cum = jnp.where(r8 < HEADS, _chunk_cumsum(lf, 1, False), _chunk_cumsum(lf, 1, True))
    grow_ref[:2 * HEADS, :] = xr[:2 * HEADS] - cum
    grow_ref[2 * HEADS:, :] = cum


def _kgate(x, g, w_t, layer, bcol, brow, tm):
    nt = SEQ // tm
    tok = BATCH * SEQ
    full = lambda shape: pl.BlockSpec(shape, lambda b, i: (0,) * len(shape))
    return pl.pallas_call(
        _kgate_kernel,
        grid=(BATCH, nt),
        in_specs=[pl.BlockSpec((1, tm, D_MODEL), lambda b, i: (b, i, 0)),
                  full((1, D_MODEL)),
                  pl.BlockSpec((1, QK_WIDTH, D_MODEL), lambda b, i: (layer, _K0 // QK_WIDTH, 0)),
                  pl.BlockSpec((1, N_GATE, D_MODEL), lambda b, i: (layer, _G0 // N_GATE, 0)),
                  full((N_GATE, 1)), full((1, 128))],
        out_specs=[pl.BlockSpec((1, tm, D_MODEL), lambda b, i: (b, i, 0)),
                   pl.BlockSpec((QK_WIDTH, tm), lambda b, i: (0, b * nt + i)),
                   pl.BlockSpec((N_GATE, tm), lambda b, i: (0, b * nt + i)),
                   pl.BlockSpec((tm, 128), lambda b, i: (b * nt + i, 0))],
        out_shape=[jax.ShapeDtypeStruct((BATCH, SEQ, D_MODEL), BF16),
                   jax.ShapeDtypeStruct((QK_WIDTH, tok), BF16),
                   jax.ShapeDtypeStruct((N_GATE, tok), F32),
                   jax.ShapeDtypeStruct((tok, 128), F32)],
        scratch_shapes=[pltpu.VMEM((QK_WIDTH + 128, D_MODEL), BF16),
                        pltpu.VMEM((128, D_MODEL), BF16)],
        compiler_params=_cparams(2),
    )(x, g, w_t, w_t, bcol, brow)


W_CAST_ROWS = 128


def _proj_kernel(*refs, shift, segmented):
    if shift:
        a_ref, w_ref, w2_ref, o_ref, w16_ref = refs
    else:
        a_ref, w_ref, o_ref, w16_ref = refs

    @pl.when((pl.program_id(1) == 0) & (pl.program_id(2) == 0))
    def _():
        for r in range(MM_TN // W_CAST_ROWS):
            lo = shift + r * W_CAST_ROWS
            hi = lo + W_CAST_ROWS
            if hi <= MM_TN:
                w = w_ref[0, lo:hi, :]
            else:
                w = jnp.concatenate([w_ref[0, lo:MM_TN, :], w2_ref[0, :hi - MM_TN, :]], axis=0)
            w16_ref[r * W_CAST_ROWS:(r + 1) * W_CAST_ROWS, :] = w.astype(BF16)

    nt = (((1,), (1,)), ((), ()))
    if segmented:
        a = a_ref[0].reshape(NSEG * PROJ_SEG_TILE, D_MODEL)
        r = lax.dot_general(a, w16_ref[...], nt, preferred_element_type=F32)
        o_ref[0] = jnp.swapaxes(r.reshape(NSEG, PROJ_SEG_TILE, MM_TN), 0, 1).astype(o_ref.dtype)
    else:
        o_ref[0] = lax.dot_general(a_ref[0], w16_ref[...], nt,
                                   preferred_element_type=F32).astype(o_ref.dtype)


def _proj(hn, w_t, layer, col0, w_col, n_col, out_dtype, segmented=False):
    shift = col0 % MM_TN
    base = col0 // MM_TN
    assert shift % 8 == 0 and MM_TN % max(shift, 1) == 0
    w_specs = [pl.BlockSpec((1, MM_TN, D_MODEL), lambda j, b, i: (layer, base + w_col(j), 0))]
    operands = [w_t]
    if shift:
        per = MM_TN // shift
        w_specs.append(pl.BlockSpec((1, shift, D_MODEL),
                                    lambda j, b, i: (layer, (base + w_col(j) + 1) * per, 0)))
        operands.append(w_t)
    if segmented:
        a = hn.reshape(BATCH, NSEG, SEG, D_MODEL)
        grid = (n_col, BATCH, SEG // PROJ_SEG_TILE)
        a_spec = pl.BlockSpec((1, NSEG, PROJ_SEG_TILE, D_MODEL), lambda j, b, i: (b, 0, i, 0))
        out_spec = pl.BlockSpec((1, PROJ_SEG_TILE, NSEG, MM_TN), lambda j, b, i: (b, i, 0, j))
        out_shape = jax.ShapeDtypeStruct((BATCH, SEG, NSEG, n_col * MM_TN), out_dtype)
    else:
        a = hn
        grid = (n_col, BATCH, SEQ // PROJ_TM)
        a_spec = pl.BlockSpec((1, PROJ_TM, D_MODEL), lambda j, b, i: (b, i, 0))
        out_spec = pl.BlockSpec((1, PROJ_TM, MM_TN), lambda j, b, i: (b, i, j))
        out_shape = jax.ShapeDtypeStruct((BATCH, SEQ, n_col * MM_TN), out_dtype)
    return pl.pallas_call(
        functools.partial(_proj_kernel, shift=shift, segmented=segmented),
        grid=grid,
        in_specs=[a_spec] + w_specs,
        out_specs=out_spec,
        out_shape=out_shape,
        scratch_shapes=[pltpu.VMEM((MM_TN, D_MODEL), BF16)],
        compiler_params=_cparams(3),
    )(a, *operands)


def _mlstm_kernel(qf_ref, ktf_ref, vf_ref, rowf_ref, colf_ref,
                  qb_ref, ktb_ref, vb_ref, rowb_ref, colb_ref,
                  hf_ref, hb_ref, ct_ref, m_ref):
    @pl.when(pl.program_id(2) == 0)
    def _():
        ct_ref[...] = jnp.zeros_like(ct_ref)
        m_ref[...] = jnp.zeros_like(m_ref)

    L = CHUNK
    jj = lax.broadcasted_iota(jnp.int32, (L, L), 0)
    ss = lax.broadcasted_iota(jnp.int32, (L, L), 1)
    lane = lax.broadcasted_iota(jnp.int32, (L, 128), 1)
    ones_blk = jnp.where(lane == 0, 1.0, 0.0).astype(BF16)

    dirs = ((qf_ref, ktf_ref, vf_ref, rowf_ref, colf_ref, hf_ref),
            (qb_ref, ktb_ref, vb_ref, rowb_ref, colb_ref, hb_ref))
    chains = [(hh, d) for hh in range(MLSTM_HPS) for d in range(2)]
    for hh, d in chains:
        q_ref, kt_ref, v_ref, row_ref, col_ref, out_ref = dirs[d]
        head = pl.program_id(1) * MLSTM_HPS + hh
        st = d * MLSTM_HPS + hh
        q = q_ref[0, :, hh * QK_DIM:(hh + 1) * QK_DIM]
        kt = kt_ref[hh * QK_DIM:(hh + 1) * QK_DIM, :]
        v_aug = jnp.concatenate([v_ref[0, :, hh * V_DIM:(hh + 1) * V_DIM], ones_blk],
                                axis=1)
        a_row = row_ref[pl.ds(HEADS * d + head, 1), :]
        b_row = row_ref[pl.ds(2 * HEADS + HEADS * d + head, 1), :]
        b_col = jnp.sum(jnp.where(lane == 2 * HEADS + HEADS * d + head, col_ref[...], 0.0),
                        axis=1, keepdims=True)
        m_prev = m_ref[st, 0:1, 0:1]

        causal = (ss <= jj) if d == 0 else (ss >= jj)
        d_log = jnp.where(causal, b_col + a_row, -jnp.inf)
        inter = b_col + m_prev
        m_row = jnp.maximum(inter, jnp.max(d_log, axis=1, keepdims=True))
        w_intra = jnp.exp(d_log - m_row)
        w_inter = jnp.exp(inter - m_row)

        s = jnp.dot(q, kt, preferred_element_type=F32)
        p = (s * w_intra).astype(BF16)
        ct = ct_ref[st]
        nd = (jnp.dot(p, v_aug, preferred_element_type=F32)
              + w_inter * jnp.dot(q, ct.astype(BF16), preferred_element_type=F32))
        num = nd[:, :V_DIM]
        den = nd[:, V_DIM:V_DIM + 1]
        out_ref[0, :, hh * V_DIM:(hh + 1) * V_DIM] = (
            num / jnp.maximum(jnp.abs(den), jnp.exp(-m_row))).astype(out_ref.dtype)

        g_tot = b_row[:, L - 1:L] if d == 0 else b_row[:, 0:1]
        w_log = g_tot + a_row
        m_new = jnp.maximum(g_tot + m_prev, jnp.max(w_log, axis=1, keepdims=True))
        w_k = jnp.exp(w_log - m_new)
        decay = jnp.exp(g_tot + m_prev - m_new)
        ktw = (kt.astype(F32) * w_k).astype(BF16)
        ct_ref[st] = decay * ct + jnp.dot(ktw, v_aug, preferred_element_type=F32)
        m_ref[st] = jnp.broadcast_to(m_new, (8, 128))


def _mlstm(head, kt, grow, gcol):
    L, nc, hps = CHUNK, N_CHUNK, MLSTM_HPS
    qw, vw = hps * QK_DIM, hps * V_DIM
    rev = lambda c: nc - 1 - c

    def specs(cmap):
        return [
            pl.BlockSpec((1, L, qw), lambda b, h, c: (b, cmap(c), HEAD_Q // qw + h)),
            pl.BlockSpec((qw, L), lambda b, h, c: (h, b * nc + cmap(c))),
            pl.BlockSpec((1, L, vw), lambda b, h, c: (b, cmap(c), HEAD_V // vw + h)),
            pl.BlockSpec((N_GATE, L), lambda b, h, c: (0, b * nc + cmap(c))),
            pl.BlockSpec((L, 128), lambda b, h, c: (b * nc + cmap(c), 0)),
        ]

    fwd = lambda c: c
    out_sds = jax.ShapeDtypeStruct((BATCH, SEQ, M_WIDTH), BF16)
    return pl.pallas_call(
        _mlstm_kernel,
        grid=(BATCH, HEADS // hps, nc),
        in_specs=specs(fwd) + specs(rev),
        out_specs=[pl.BlockSpec((1, L, vw), lambda b, h, c: (b, c, h)),
                   pl.BlockSpec((1, L, vw), lambda b, h, c: (b, rev(c), h))],
        out_shape=[out_sds, out_sds],
        scratch_shapes=[pltpu.VMEM((2 * hps, QK_DIM, V_AUG), F32),
                        pltpu.VMEM((2 * hps, 8, 128), F32)],
        compiler_params=_cparams(3),
    )(head, kt, head, grow, gcol, head, kt, head, grow, gcol)


def _lru_kernel(x_ref, z_ref, cw_ref, cb_ref, w_ref, b_ref, lam_ref, o_ref,
                xs_ref, hf_ref, pf_ref, hb_ref, pb_ref, wh_ref, a_ref, u_ref):
    wc = x_ref.shape[-1]
    nb = wc // LRU_BLOCK
    tc = LRU_TC
    n_it = SEG // tc
    sub = lax.broadcasted_iota(jnp.int32, (NSEG, wc), 0)

    def copy_body(it, carry):
        t0 = pl.multiple_of(it * tc, tc)
        xs_ref[pl.ds(t0 + 2, tc)] = x_ref[0, pl.ds(t0, tc)]
        return carry

    lax.fori_loop(0, n_it, copy_body, 0)
    for r in range(2):
        prev = pltpu.roll(x_ref[0, SEG - 2 + r], 1, 0)
        xs_ref[r] = jnp.where(sub >= 1, prev, 0.0)
    nxt = pltpu.roll(x_ref[0, 0], NSEG - 1, 0)
    xs_ref[SEG + 2] = jnp.where(sub <= NSEG - 2, nxt, 0.0)

    cw = [cw_ref[t:t + 1, :][None] for t in range(CONV_W)]
    cb = cb_ref[...][None]

    def conv_body(it, carry):
        t0 = pl.multiple_of(it * tc, tc)
        acc = xs_ref[pl.ds(t0, tc)] * cw[0]
        for t in range(1, CONV_W):
            acc = acc + xs_ref[pl.ds(t0 + t, tc)] * cw[t]
        xs_ref[pl.ds(t0, tc)] = cb + acc
        return carry

    lax.fori_loop(0, n_it, conv_body, 0)

    wh_ref[...] = jnp.zeros_like(wh_ref)
    wh_ref[:, :, :LRU_BLOCK, :] = (0.5 * w_ref[...].astype(F32)).astype(BF16)
    bh = 0.5 * b_ref[...]
    b_hi = bh.astype(BF16)
    b_lo = (bh - b_hi.astype(F32)).astype(BF16)
    wh_ref[:, :, LRU_BLOCK:LRU_BLOCK + 1, :] = b_hi
    wh_ref[:, :, LRU_BLOCK + 1:LRU_BLOCK + 2, :] = b_lo
    lane_blk = lax.broadcasted_iota(jnp.int32, (tc * NSEG, LRU_BLOCK), 1)
    ones_blk = jnp.where(lane_blk < 2, 1.0, 0.0).astype(BF16)
    hcs = (-0.5 * LRU_C) * _softplus(-lam_ref[...])

    def conv(t0):
        return xs_ref[pl.ds(t0, tc)]

    def gates(xc, d):
        x2 = xc.reshape(tc * NSEG, wc)
        x16 = x2.astype(BF16)
        hx = 0.5 * x2
        a_parts, u_parts = [], []
        for j in range(nb):
            cols = slice(j * LRU_BLOCK, (j + 1) * LRU_BLOCK)
            lhs = jnp.concatenate([x16[:, cols], ones_blk], axis=1)
            pre = jnp.dot(lhs, wh_ref[d, j], preferred_element_type=F32)
            t_r = jnp.tanh(pre[:, :LRU_BLOCK])
            t_i = jnp.tanh(pre[:, LRU_BLOCK:])
            h = hcs[d:d + 1, cols]
            log_a = h * t_r + h
            a_parts.append(jnp.exp(log_a))
            th = jnp.tanh(log_a)
            p = -2.0 * th
            q = 1.0 - th
            coef = jnp.where(p > 0.0, p * lax.rsqrt(p * q), 0.0)
            u_parts.append(coef * ((t_i + 1.0) * hx[:, cols]))
        a = jnp.concatenate(a_parts, axis=1).reshape(tc, NSEG, wc)
        u = jnp.concatenate(u_parts, axis=1).reshape(tc, NSEG, wc)
        return a, u

    def chunk_of(it, d):
        return it if d == 0 else n_it - 1 - it

    def gates_to(slot, it, d):
        t0 = pl.multiple_of(chunk_of(it, d) * tc, tc)
        a, u = gates(conv(t0), d)
        a_ref[d, slot] = a
        u_ref[d, slot] = u

    def half_step(it, carry, slot):
        cur = [(a_ref[d, slot], u_ref[d, slot]) for d in range(2)]
        nxt = jnp.minimum(it + 1, n_it - 1)
        for d in range(2):
            gates_to(1 - slot, nxt, d)
        out = []
        for d, (h_ref, p_ref) in enumerate(((hf_ref, pf_ref), (hb_ref, pb_ref))):
            h, p = carry[2 * d], carry[2 * d + 1]
            a, u = cur[d]
            t0 = pl.multiple_of(chunk_of(it, d) * tc, tc)
            hs, ps = [None] * tc, [None] * tc
            order = range(tc) if d == 0 else range(tc - 1, -1, -1)
            for k in order:
                h = a[k] * h + u[k]
                p = a[k] * p
                hs[k], ps[k] = h, p
            h_ref[pl.ds(t0, tc)] = jnp.stack(hs)
            p_ref[pl.ds(t0, tc)] = jnp.stack(ps)
            out += [h, p]
        return tuple(out)

    def scan_body(i2, carry):
        carry = half_step(2 * i2, carry, 0)
        return half_step(2 * i2 + 1, carry, 1)

    for d in range(2):
        gates_to(0, 0, d)
    zeros = jnp.zeros((NSEG, wc), F32)
    ones = jnp.ones((NSEG, wc), F32)
    hf_end, pf_end, hb_end, pb_end = lax.fori_loop(0, n_it // 2, scan_body,
                                                   (zeros, ones, zeros, ones))

    def carry_in(h_end, p_end, d):
        cin = jnp.zeros((NSEG, wc), F32)
        c = jnp.zeros((1, wc), F32)
        order = range(NSEG) if d == 0 else range(NSEG - 1, -1, -1)
        for s in order:
            cin = jnp.where(sub == s, c, cin)
            c = h_end[s:s + 1] + p_end[s:s + 1] * c
        return cin

    cin_f = carry_in(hf_end, pf_end, 0)
    cin_b = carry_in(hb_end, pb_end, 1)

    def out_body(it, carry):
        rows = pl.ds(pl.multiple_of(it * tc, tc), tc)
        z = z_ref[0, rows]
        h = (hf_ref[rows] + pf_ref[rows] * cin_f) + (hb_ref[rows] + pb_ref[rows] * cin_b)
        o_ref[0, rows] = (h * (z * _sigmoid(z))).astype(o_ref.dtype)
        return carry

    lax.fori_loop(0, n_it, out_body, 0)


def _lru(xz, cw, cb, w, b, lam):
    wc = LRU_WC
    nb = wc // LRU_BLOCK
    nj = LRU_W // wc
    blk = (1, SEG, NSEG, wc)
    seg_buf = pltpu.VMEM((SEG, NSEG, wc), F32)
    return pl.pallas_call(
        _lru_kernel,
        grid=(BATCH, nj),
        in_specs=[pl.BlockSpec(blk, lambda b_, j: (b_, 0, 0, j)),
                  pl.BlockSpec(blk, lambda b_, j: (b_, 0, 0, nj + j)),
                  pl.BlockSpec((CONV_W, wc), lambda b_, j: (0, j)),
                  pl.BlockSpec((1, wc), lambda b_, j: (0, j)),
                  pl.BlockSpec((2, nb, LRU_BLOCK, 2 * LRU_BLOCK), lambda b_, j: (0, j, 0, 0)),
                  pl.BlockSpec((2, nb, 1, 2 * LRU_BLOCK), lambda b_, j: (0, j, 0, 0)),
                  pl.BlockSpec((2, wc), lambda b_, j: (0, j))],
        out_specs=pl.BlockSpec(blk, lambda b_, j: (b_, 0, 0, j)),
        out_shape=jax.ShapeDtypeStruct((BATCH, SEG, NSEG, LRU_W), F32),
        scratch_shapes=[pltpu.VMEM((SEG + 3, NSEG, wc), F32),
                        seg_buf, seg_buf, seg_buf, seg_buf,
                        pltpu.VMEM((2, nb, 2 * LRU_BLOCK, 2 * LRU_BLOCK), BF16),
                        pltpu.VMEM((2, 2, LRU_TC, NSEG, wc), F32),
                        pltpu.VMEM((2, 2, LRU_TC, NSEG, wc), F32)],
        compiler_params=_cparams(2),
    )(xz, xz, cw, cb, w, b, lam)


def _branch_a_kernel(hf_ref, hb_ref, o_ref, za_ref, hg_ref, ga_ref, w_ref, out_ref, ya_ref):
    tm = ya_ref.shape[0]
    acc = None
    for hh in range(HEADS):
        cols = slice(hh * V_DIM, (hh + 1) * V_DIM)
        hg = hg_ref[:, cols]
        for r in range(tm // NORM_ROWS):
            rows = slice(r * NORM_ROWS, (r + 1) * NORM_ROWS)
            hs = hf_ref[0, rows, cols].astype(F32) + hb_ref[0, rows, cols].astype(F32)
            ms = jnp.mean(hs * hs, axis=-1, keepdims=True)
            hn = hs * lax.rsqrt(ms + NORM_EPS) * hg
            o = o_ref[0, rows, cols].astype(F32)
            z = za_ref[0, rows, cols].astype(F32)
            gate = (0.25 * z) * ((1.0 + jnp.tanh(0.5 * o)) * (1.0 + jnp.tanh(0.5 * z)))
            ya_ref[rows, cols] = (hn * gate).astype(BF16)
        part = jnp.dot(ya_ref[:, cols], w_ref[0, cols, :], preferred_element_type=F32)
        acc = part if acc is None else acc + part
    out_ref[0] = (_sigmoid(ga_ref[0].astype(F32)) * acc).astype(out_ref.dtype)


def _branch_a(hf, hb, head, gg, hg, w, layer, tm):
    wide = (1, tm, M_WIDTH)
    return pl.pallas_call(
        _branch_a_kernel,
        grid=(BATCH, SEQ // tm),
        in_specs=[pl.BlockSpec(wide, lambda b, i: (b, i, 0)),
                  pl.BlockSpec(wide, lambda b, i: (b, i, 0)),
                  pl.BlockSpec(wide, lambda b, i: (b, i, HEAD_O // M_WIDTH)),
                  pl.BlockSpec(wide, lambda b, i: (b, i, HEAD_ZA // M_WIDTH)),
                  pl.BlockSpec((1, M_WIDTH), lambda b, i: (0, 0)),
                  pl.BlockSpec((1, tm, D_MODEL), lambda b, i: (b, i, 0)),
                  pl.BlockSpec((1, M_WIDTH, D_MODEL), lambda b, i: (layer, 0, 0))],
        out_specs=pl.BlockSpec((1, tm, D_MODEL), lambda b, i: (b, i, 0)),
        out_shape=jax.ShapeDtypeStruct((BATCH, SEQ, D_MODEL), BF16),
        scratch_shapes=[pltpu.VMEM((tm, M_WIDTH), BF16)],
        compiler_params=_cparams(2),
    )(hf, hb, head, head, hg, gg, w)


def _branch_b_kernel(yb_ref, gb_ref, pa_ref, w_ref, out_ref, y16_ref):
    rows = NSEG * SEG_TILE
    step = 512
    for c in range(LRU_W // step):
        cols = slice(c * step, (c + 1) * step)
        y = jnp.swapaxes(yb_ref[0, :, :, cols], 0, 1)
        y16_ref[:, cols] = y.reshape(rows, step).astype(BF16)
    acc = jnp.dot(y16_ref[...], w_ref[0], preferred_element_type=F32)
    pa = pa_ref[0].reshape(rows, D_MODEL).astype(F32)
    gb = gb_ref[0].reshape(rows, D_MODEL).astype(F32)
    out_ref[0] = (pa + _sigmoid(gb) * acc).astype(out_ref.dtype).reshape(NSEG, SEG_TILE, D_MODEL)


def _branch_b(yb, gg, part_a, w, layer):
    seg4 = lambda arr: arr.reshape(BATCH, NSEG, SEG, arr.shape[-1])
    tile = (1, NSEG, SEG_TILE, D_MODEL)
    merged = pl.pallas_call(
        _branch_b_kernel,
        grid=(BATCH, SEG // SEG_TILE),
        in_specs=[pl.BlockSpec((1, SEG_TILE, NSEG, LRU_W), lambda b, i: (b, i, 0, 0)),
                  pl.BlockSpec(tile, lambda b, i: (b, 0, i, 1)),
                  pl.BlockSpec(tile, lambda b, i: (b, 0, i, 0)),
                  pl.BlockSpec((1, LRU_W, D_MODEL), lambda b, i: (layer, 0, 0))],
        out_specs=pl.BlockSpec(tile, lambda b, i: (b, 0, i, 0)),
        out_shape=jax.ShapeDtypeStruct((BATCH, NSEG, SEG, D_MODEL), BF16),
        scratch_shapes=[pltpu.VMEM((NSEG * SEG_TILE, LRU_W), BF16)],
        compiler_params=_cparams(2),
    )(yb, seg4(gg), seg4(part_a), w)
    return merged.reshape(BATCH, SEQ, D_MODEL)


def _out_kernel(m_ref, x_ref, w_ref, fg_ref, o_ref, *, final_norm):
    y = x_ref[0] + jnp.dot(m_ref[0], w_ref[0], preferred_element_type=F32)
    if final_norm:
        ms = jnp.mean(y * y, axis=-1, keepdims=True)
        y = y * lax.rsqrt(ms + NORM_EPS) * fg_ref[...]
    o_ref[0] = y


def _out_proj(merged, x, w, layer, fg, final_norm, tm):
    row = lambda b, i: (b, i, 0)
    return pl.pallas_call(
        functools.partial(_out_kernel, final_norm=final_norm),
        grid=(BATCH, SEQ // tm),
        in_specs=[pl.BlockSpec((1, tm, D_MODEL), row),
                  pl.BlockSpec((1, tm, D_MODEL), row),
                  pl.BlockSpec((1, D_MODEL, D_MODEL), lambda b, i: (layer, 0, 0)),
                  pl.BlockSpec((1, D_MODEL), lambda b, i: (0, 0))],
        out_specs=pl.BlockSpec((1, tm, D_MODEL), row),
        out_shape=jax.ShapeDtypeStruct((BATCH, SEQ, D_MODEL), F32),
        compiler_params=_cparams(2),
    )(merged, x, w, fg)


def kernel(x, norm_g, w_in, b_if, head_g, conv_w, conv_b, w_rg, b_rg, lru_lambda,
           w_branch_a, w_branch_b, w_out, final_g):
    w_t = jnp.swapaxes(w_in, 1, 2)
    bg = jnp.pad(b_if, ((0, 0), (0, 128 - N_GATE)))
    n_blk = LRU_W // LRU_BLOCK
    w_gate = jnp.transpose(w_rg, (0, 1, 3, 4, 2, 5)).reshape(
        DEPTH, 2, n_blk, LRU_BLOCK, 2 * LRU_BLOCK).astype(BF16)
    b_gate = jnp.transpose(b_rg.reshape(DEPTH, 2, 2, n_blk, LRU_BLOCK), (0, 1, 3, 2, 4)).reshape(
        DEPTH, 2, n_blk, 1, 2 * LRU_BLOCK)
    w_a16 = w_branch_a.astype(BF16)
    w_b16 = w_branch_b.astype(BF16)
    w_o16 = w_out.astype(BF16)

    head_col = lambda j: jnp.where(j < 4, j + 4, jnp.where(j < 6, j - 2, 0))
    xz_cols = 2 * LRU_W // MM_TN

    h = x
    for l in range(DEPTH):
        hn, kt, grow, gcol = _kgate(h, norm_g[l][None, :], w_t, l, b_if[l][:, None],
                                    bg[l][None, :], tm=512)
        head = _proj(hn, w_t, l, 0, head_col, N_HEAD // MM_TN, BF16)
        gg = _proj(hn, w_t, l, _TAIL0, lambda j: xz_cols + j, 2 * D_MODEL // MM_TN, BF16)
        xz = _proj(hn, w_t, l, _TAIL0, lambda j: j, xz_cols, F32, segmented=True)

        hf, hb = _mlstm(head, kt, grow, gcol)
        yb = _lru(xz, conv_w[l], conv_b[l][None, :], w_gate[l], b_gate[l], lru_lambda[l])

        part_a = _branch_a(hf, hb, head, gg, head_g[l][None, :], w_a16, l, tm=512)
        merged = _branch_b(yb, gg, part_a, w_b16, l)
        h = _out_proj(merged, h, w_o16, l, final_g[None, :], l == DEPTH - 1, tm=512)
    return h
```

```python
import functools

import jax
import jax.numpy as jnp
import numpy as np
from jax import lax
from jax.experimental import pallas as pl
from jax.experimental.pallas import tpu as pltpu

F32 = jnp.float32
BF16 = jnp.bfloat16

D_MODEL = 2048
BATCH = 4
SEQ = 4096
DEPTH = 2
HEADS = 4
QK_DIM = 256
V_DIM = 512
QK_WIDTH = HEADS * QK_DIM
M_WIDTH = HEADS * V_DIM
N_GATE = 4 * HEADS
LRU_W = D_MODEL
LRU_BLOCK = 128
LRU_C = 8.0
CONV_W = 4
NORM_EPS = 1e-6
QK_SCALE = QK_DIM ** -0.5

_K0, _V0 = 1024, 2048
_G0 = 8192
_TAIL0 = _G0 + N_GATE

HEAD_O, HEAD_ZA, HEAD_V, HEAD_Q = 0, 2048, 4096, 6144
N_HEAD = 7168

CHUNK = 256
N_CHUNK = SEQ // CHUNK
V_AUG = V_DIM + 128
MLSTM_HPS = 4

NSEG = 8
SEG = SEQ // NSEG
LRU_WC = 256
LRU_TC = 32

SEG_TILE = 64
PROJ_SEG_TILE = 128
KG_ROWS = QK_WIDTH + 16 * HEADS

NORM_ROWS = 128
MM_TN = 1024
PROJ_TM = 2048
VMEM_LIMIT = 56 * 1024 * 1024


def _cparams(n_axes):
    return pltpu.CompilerParams(dimension_semantics=("arbitrary",) * n_axes,
                                vmem_limit_bytes=VMEM_LIMIT)


def _sigmoid(x):
    return 0.5 * jnp.tanh(0.5 * x) + 0.5


def _log_sigmoid(x):
    return jnp.minimum(x, 0.0) - jnp.log1p(jnp.exp(-jnp.abs(x)))


def _softplus(x):
    return jnp.maximum(x, 0.0) + jnp.log1p(jnp.exp(-jnp.abs(x)))


def _chunk_cumsum(x, axis, reverse):
    n = x.shape[axis]
    idx = lax.broadcasted_iota(jnp.int32, x.shape, axis) % CHUNK
    d = 1
    while d < CHUNK:
        if reverse:
            x = x + jnp.where(idx < CHUNK - d, pltpu.roll(x, n - d, axis), 0.0)
        else:
            x = x + jnp.where(idx >= d, pltpu.roll(x, d, axis), 0.0)
        d *= 2
    return x


def _kgate_kernel(x_ref, g_ref, wk_ref, wg_ref, bcol_ref, brow_ref,
                  hn_ref, kt_ref, grow_ref, gcol_ref, wkt_ref, wg16_ref):
    @pl.when((pl.program_id(0) == 0) & (pl.program_id(1) == 0))
    def _():
        step = 256
        for c in range(QK_WIDTH // step):
            wkt_ref[c * step:(c + 1) * step, :] = wk_ref[0, c * step:(c + 1) * step, :].astype(BF16)
        wg16_ref[...] = jnp.zeros_like(wg16_ref)
        wg16_ref[:N_GATE, :] = wg_ref[0].astype(BF16)
        wkt_ref[QK_WIDTH:, :] = wg16_ref[...]

    g = g_ref[...]

    def norm_body(r, carry):
        rows = pl.ds(pl.multiple_of(r * NORM_ROWS, NORM_ROWS), NORM_ROWS)
        x = x_ref[0, rows, :]
        ms = jnp.mean(x * x, axis=-1, keepdims=True)
        hn_ref[0, rows, :] = (x * lax.rsqrt(ms + NORM_EPS) * g).astype(BF16)
        return carry

    lax.fori_loop(0, hn_ref.shape[1] // NORM_ROWS, norm_body, 0)
    hn = hn_ref[0]
    nt = (((1,), (1,)), ((), ()))
    kg = lax.dot_general(wkt_ref[...], hn, nt, preferred_element_type=F32)
    kt_ref[...] = (kg[:QK_WIDTH] * QK_SCALE).astype(BF16)

    wg = wg16_ref[...]
    xc = lax.dot_general(hn, wg, nt, preferred_element_type=F32) + brow_ref[...]
    lfc = _log_sigmoid(xc)
    lane = lax.broadcasted_iota(jnp.int32, lfc.shape, 1)
    gcol_ref[...] = jnp.where(lane < 3 * HEADS, _chunk_cumsum(lfc, 0, False),
                              _chunk_cumsum(lfc, 0, True))

    xr = kg[QK_WIDTH:QK_WIDTH + N_GATE] + bcol_ref[...]
    lf = _log_sigmoid(xr[2 * HEADS:])
    r8 = lax.broadcasted_iota(jnp.int32, lf.shape, 0)
    cum = jnp.where(r8 < HEADS, _chunk_cumsum(lf, 1, False), _chunk_cumsum(lf, 1, True))
    grow_ref[:2 * HEADS, :] = xr[:2 * HEADS] - cum
    grow_ref[2 * HEADS:, :] = cum


def _kgate(x, g, w_t, layer, bcol, brow, tm):
    nt = SEQ // tm
    tok = BATCH * SEQ
    full = lambda shape: pl.BlockSpec(shape, lambda b, i: (0,) * len(shape))
    return pl.pallas_call(
        _kgate_kernel,
        grid=(BATCH, nt),
        in_specs=[pl.BlockSpec((1, tm, D_MODEL), lambda b, i: (b, i, 0)),
                  full((1, D_MODEL)),
                  pl.BlockSpec((1, QK_WIDTH, D_MODEL), lambda b, i: (layer, _K0 // QK_WIDTH, 0)),
                  pl.BlockSpec((1, N_GATE, D_MODEL), lambda b, i: (layer, _G0 // N_GATE, 0)),
                  full((N_GATE, 1)), full((1, 128))],
        out_specs=[pl.BlockSpec((1, tm, D_MODEL), lambda b, i: (b, i, 0)),
                   pl.BlockSpec((QK_WIDTH, tm), lambda b, i: (0, b * nt + i)),
                   pl.BlockSpec((N_GATE, tm), lambda b, i: (0, b * nt + i)),
                   pl.BlockSpec((tm, 128), lambda b, i: (b * nt + i, 0))],
        out_shape=[jax.ShapeDtypeStruct((BATCH, SEQ, D_MODEL), BF16),
                   jax.ShapeDtypeStruct((QK_WIDTH, tok), BF16),
                   jax.ShapeDtypeStruct((N_GATE, tok), F32),
                   jax.ShapeDtypeStruct((tok, 128), F32)],
        scratch_shapes=[pltpu.VMEM((QK_WIDTH + 128, D_MODEL), BF16),
                        pltpu.VMEM((128, D_MODEL), BF16)],
        compiler_params=_cparams(2),
    )(x, g, w_t, w_t, bcol, brow)


W_CAST_ROWS = 128


def _proj_kernel(*refs, shift, segmented):
    if shift:
        a_ref, w_ref, w2_ref, o_ref, w16_ref = refs
    else:
        a_ref, w_ref, o_ref, w16_ref = refs

    @pl.when((pl.program_id(1) == 0) & (pl.program_id(2) == 0))
    def _():
        for r in range(MM_TN // W_CAST_ROWS):
            lo = shift + r * W_CAST_ROWS
            hi = lo + W_CAST_ROWS
            if hi <= MM_TN:
                w = w_ref[0, lo:hi, :]
            else:
                w = jnp.concatenate([w_ref[0, lo:MM_TN, :], w2_ref[0, :hi - MM_TN, :]], axis=0)
            w16_ref[r * W_CAST_ROWS:(r + 1) * W_CAST_ROWS, :] = w.astype(BF16)

    nt = (((1,), (1,)), ((), ()))
    if segmented:
        a = a_ref[0].reshape(NSEG * PROJ_SEG_TILE, D_MODEL)
        r = lax.dot_general(a, w16_ref[...], nt, preferred_element_type=F32)
        o_ref[0] = jnp.swapaxes(r.reshape(NSEG, PROJ_SEG_TILE, MM_TN), 0, 1).astype(o_ref.dtype)
    else:
        o_ref[0] = lax.dot_general(a_ref[0], w16_ref[...], nt,
                                   preferred_element_type=F32).astype(o_ref.dtype)


def _proj(hn, w_t, layer, col0, w_col, n_col, out_dtype, segmented=False):
    shift = col0 % MM_TN
    base = col0 // MM_TN
    assert shift % 8 == 0 and MM_TN % max(shift, 1) == 0
    w_specs = [pl.BlockSpec((1, MM_TN, D_MODEL), lambda j, b, i: (layer, base + w_col(j), 0))]
    operands = [w_t]
    if shift:
        per = MM_TN // shift
        w_specs.append(pl.BlockSpec((1, shift, D_MODEL),
                                    lambda j, b, i: (layer, (base + w_col(j) + 1) * per, 0)))
        operands.append(w_t)
    if segmented:
        a = hn.reshape(BATCH, NSEG, SEG, D_MODEL)
        grid = (n_col, BATCH, SEG // PROJ_SEG_TILE)
        a_spec = pl.BlockSpec((1, NSEG, PROJ_SEG_TILE, D_MODEL), lambda j, b, i: (b, 0, i, 0))
        out_spec = pl.BlockSpec((1, PROJ_SEG_TILE, NSEG, MM_TN), lambda j, b, i: (b, i, 0, j))
        out_shape = jax.ShapeDtypeStruct((BATCH, SEG, NSEG, n_col * MM_TN), out_dtype)
    else:
        a = hn
        grid = (n_col, BATCH, SEQ // PROJ_TM)
        a_spec = pl.BlockSpec((1, PROJ_TM, D_MODEL), lambda j, b, i: (b, i, 0))
        out_spec = pl.BlockSpec((1, PROJ_TM, MM_TN), lambda j, b, i: (b, i, j))
        out_shape = jax.ShapeDtypeStruct((BATCH, SEQ, n_col * MM_TN), out_dtype)
    return pl.pallas_call(
        functools.partial(_proj_kernel, shift=shift, segmented=segmented),
        grid=grid,
        in_specs=[a_spec] + w_specs,
        out_specs=out_spec,
        out_shape=out_shape,
        scratch_shapes=[pltpu.VMEM((MM_TN, D_MODEL), BF16)],
        compiler_params=_cparams(3),
    )(a, *operands)


def _mlstm_kernel(qf_ref, ktf_ref, vf_ref, rowf_ref, colf_ref,
                  qb_ref, ktb_ref, vb_ref, rowb_ref, colb_ref,
                  hf_ref, hb_ref, ct_ref, m_ref):
    @pl.when(pl.program_id(2) == 0)
    def _():
        ct_ref[...] = jnp.zeros_like(ct_ref)
        m_ref[...] = jnp.zeros_like(m_ref)

    L = CHUNK
    jj = lax.broadcasted_iota(jnp.int32, (L, L), 0)
    ss = lax.broadcasted_iota(jnp.int32, (L, L), 1)
    lane = lax.broadcasted_iota(jnp.int32, (L, 128), 1)
    ones_blk = jnp.where(lane == 0, 1.0, 0.0).astype(BF16)

    dirs = ((qf_ref, ktf_ref, vf_ref, rowf_ref, colf_ref, hf_ref),
            (qb_ref, ktb_ref, vb_ref, rowb_ref, colb_ref, hb_ref))
    chains = [(hh, d) for hh in range(MLSTM_HPS) for d in range(2)]
    for hh, d in chains:
        q_ref, kt_ref, v_ref, row_ref, col_ref, out_ref = dirs[d]
        head = pl.program_id(1) * MLSTM_HPS + hh
        st = d * MLSTM_HPS + hh
        q = q_ref[0, :, hh * QK_DIM:(hh + 1) * QK_DIM]
        kt = kt_ref[hh * QK_DIM:(hh + 1) * QK_DIM, :]
        v_aug = jnp.concatenate([v_ref[0, :, hh * V_DIM:(hh + 1) * V_DIM], ones_blk],
                                axis=1)
        a_row = row_ref[pl.ds(HEADS * d + head, 1), :]
        b_row = row_ref[pl.ds(2 * HEADS + HEADS * d + head, 1), :]
        b_col = jnp.sum(jnp.where(lane == 2 * HEADS + HEADS * d + head, col_ref[...], 0.0),
                        axis=1, keepdims=True)
        m_prev = m_ref[st, 0:1, 0:1]

        causal = (ss <= jj) if d == 0 else (ss >= jj)
        d_log = jnp.where(causal, b_col + a_row, -jnp.inf)
        inter = b_col + m_prev
        m_row = jnp.maximum(inter, jnp.max(d_log, axis=1, keepdims=True))
        w_intra = jnp.exp(d_log - m_row)
        w_inter = jnp.exp(inter - m_row)

        s = jnp.dot(q, kt, preferred_element_type=F32)
        p = (s * w_intra).astype(BF16)
        ct = ct_ref[st]
        nd = (jnp.dot(p, v_aug, preferred_element_type=F32)
              + w_inter * jnp.dot(q, ct.astype(BF16), preferred_element_type=F32))
        num = nd[:, :V_DIM]
        den = nd[:, V_DIM:V_DIM + 1]
        inv = 1.0 / jnp.maximum(jnp.abs(den), jnp.exp(-m_row))
        out_ref[0, :, hh * V_DIM:(hh + 1) * V_DIM] = (num * inv).astype(out_ref.dtype)

        g_tot = b_row[:, L - 1:L] if d == 0 else b_row[:, 0:1]
        w_log = g_tot + a_row
        m_new = jnp.maximum(g_tot + m_prev, jnp.max(w_log, axis=1, keepdims=True))
        w_k = jnp.exp(w_log - m_new)
        decay = jnp.exp(g_tot + m_prev - m_new)
        ktw = (kt.astype(F32) * w_k).astype(BF16)
        ct_ref[st] = decay * ct + jnp.dot(ktw, v_aug, preferred_element_type=F32)
        m_ref[st] = jnp.broadcast_to(m_new, (8, 128))


def _mlstm(head, kt, grow, gcol):
    L, nc, hps = CHUNK, N_CHUNK, MLSTM_HPS
    qw, vw = hps * QK_DIM, hps * V_DIM
    rev = lambda c: nc - 1 - c

    def specs(cmap):
        return [
            pl.BlockSpec((1, L, qw), lambda b, h, c: (b, cmap(c), HEAD_Q // qw + h)),
            pl.BlockSpec((qw, L), lambda b, h, c: (h, b * nc + cmap(c))),
            pl.BlockSpec((1, L, vw), lambda b, h, c: (b, cmap(c), HEAD_V // vw + h)),
            pl.BlockSpec((N_GATE, L), lambda b, h, c: (0, b * nc + cmap(c))),
            pl.BlockSpec((L, 128), lambda b, h, c: (b * nc + cmap(c), 0)),
        ]

    fwd = lambda c: c
    out_sds = jax.ShapeDtypeStruct((BATCH, SEQ, M_WIDTH), BF16)
    return pl.pallas_call(
        _mlstm_kernel,
        grid=(BATCH, HEADS // hps, nc),
        in_specs=specs(fwd) + specs(rev),
        out_specs=[pl.BlockSpec((1, L, vw), lambda b, h, c: (b, c, h)),
                   pl.BlockSpec((1, L, vw), lambda b, h, c: (b, rev(c), h))],
        out_shape=[out_sds, out_sds],
        scratch_shapes=[pltpu.VMEM((2 * hps, QK_DIM, V_AUG), F32),
                        pltpu.VMEM((2 * hps, 8, 128), F32)],
        compiler_params=_cparams(3),
    )(head, kt, head, grow, gcol, head, kt, head, grow, gcol)


def _lru_kernel(x_ref, z_ref, cw_ref, cb_ref, w_ref, b_ref, lam_ref, o_ref,
                xs_ref, hf_ref, pf_ref, hb_ref, pb_ref, wh_ref, a_ref, u_ref):
    wc = x_ref.shape[-1]
    nb = wc // LRU_BLOCK
    tc = LRU_TC
    n_it = SEG // tc
    sub = lax.broadcasted_iota(jnp.int32, (NSEG, wc), 0)

    def copy_body(it, carry):
        t0 = pl.multiple_of(it * tc, tc)
        xs_ref[pl.ds(t0 + 2, tc)] = x_ref[0, pl.ds(t0, tc)]
        return carry

    lax.fori_loop(0, n_it, copy_body, 0)
    for r in range(2):
        prev = pltpu.roll(x_ref[0, SEG - 2 + r], 1, 0)
        xs_ref[r] = jnp.where(sub >= 1, prev, 0.0)
    nxt = pltpu.roll(x_ref[0, 0], NSEG - 1, 0)
    xs_ref[SEG + 2] = jnp.where(sub <= NSEG - 2, nxt, 0.0)

    cw = [cw_ref[t:t + 1, :][None] for t in range(CONV_W)]
    cb = cb_ref[...][None]

    def conv_body(it, carry):
        t0 = pl.multiple_of(it * tc, tc)
        acc = xs_ref[pl.ds(t0, tc)] * cw[0]
        for t in range(1, CONV_W):
            acc = acc + xs_ref[pl.ds(t0 + t, tc)] * cw[t]
        xs_ref[pl.ds(t0, tc)] = cb + acc
        return carry

    lax.fori_loop(0, n_it, conv_body, 0)

    wh_ref[...] = jnp.zeros_like(wh_ref)
    wh_ref[:, :, :LRU_BLOCK, :] = (0.5 * w_ref[...].astype(F32)).astype(BF16)
    bh = 0.5 * b_ref[...]
    b_hi = bh.astype(BF16)
    b_lo = (bh - b_hi.astype(F32)).astype(BF16)
    wh_ref[:, :, LRU_BLOCK:LRU_BLOCK + 1, :] = b_hi
    wh_ref[:, :, LRU_BLOCK + 1:LRU_BLOCK + 2, :] = b_lo
    lane_blk = lax.broadcasted_iota(jnp.int32, (tc * NSEG, LRU_BLOCK), 1)
    ones_blk = jnp.where(lane_blk < 2, 1.0, 0.0).astype(BF16)
    hcs = (-0.5 * LRU_C) * _softplus(-lam_ref[...])

    def conv(t0):
        return xs_ref[pl.ds(t0, tc)]

    def gates(xc, d):
        x2 = xc.reshape(tc * NSEG, wc)
        x16 = x2.astype(BF16)
        hx = 0.5 * x2
        a_parts, u_parts = [], []
        for j in range(nb):
            cols = slice(j * LRU_BLOCK, (j + 1) * LRU_BLOCK)
            lhs = jnp.concatenate([x16[:, cols], ones_blk], axis=1)
            pre = jnp.dot(lhs, wh_ref[d, j], preferred_element_type=F32)
            t_r = jnp.tanh(pre[:, :LRU_BLOCK])
            t_i = jnp.tanh(pre[:, LRU_BLOCK:])
            h = hcs[d:d + 1, cols]
            log_a = h * t_r + h
            a_parts.append(jnp.exp(log_a))
            th = jnp.tanh(log_a)
            p = -2.0 * th
            q = 1.0 - th
            coef = jnp.where(p > 0.0, p * lax.rsqrt(p * q), 0.0)
            u_parts.append(coef * ((t_i + 1.0) * hx[:, cols]))
        a = jnp.concatenate(a_parts, axis=1).reshape(tc, NSEG, wc)
        u = jnp.concatenate(u_parts, axis=1).reshape(tc, NSEG, wc)
        return a, u

    def chunk_of(it, d):
        return it if d == 0 else n_it - 1 - it

    def gates_to(slot, it, d):
        t0 = pl.multiple_of(chunk_of(it, d) * tc, tc)
        a, u = gates(conv(t0), d)
        a_ref[d, slot] = a
        u_ref[d, slot] = u

    def half_step(it, carry, slot):
        cur = [(a_ref[d, slot], u_ref[d, slot]) for d in range(2)]
        nxt = jnp.minimum(it + 1, n_it - 1)
        for d in range(2):
            gates_to(1 - slot, nxt, d)
        out = []
        for d, (h_ref, p_ref) in enumerate(((hf_ref, pf_ref), (hb_ref, pb_ref))):
            h, p = carry[2 * d], carry[2 * d + 1]
            a, u = cur[d]
            t0 = pl.multiple_of(chunk_of(it, d) * tc, tc)
            hs, ps = [None] * tc, [None] * tc
            order = range(tc) if d == 0 else range(tc - 1, -1, -1)
            for k in order:
                h = a[k] * h + u[k]
                p = a[k] * p
                hs[k], ps[k] = h, p
            h_ref[pl.ds(t0, tc)] = jnp.stack(hs)
            p_ref[pl.ds(t0, tc)] = jnp.stack(ps)
            out += [h, p]
        return tuple(out)

    def scan_body(i2, carry):
        carry = half_step(2 * i2, carry, 0)
        return half_step(2 * i2 + 1, carry, 1)

    for d in range(2):
        gates_to(0, 0, d)
    zeros = jnp.zeros((NSEG, wc), F32)
    ones = jnp.ones((NSEG, wc), F32)
    hf_end, pf_end, hb_end, pb_end = lax.fori_loop(0, n_it // 2, scan_body,
                                                   (zeros, ones, zeros, ones))

    def carry_in(h_end, p_end, d):
        cin = jnp.zeros((NSEG, wc), F32)
        c = jnp.zeros((1, wc), F32)
        order = range(NSEG) if d == 0 else range(NSEG - 1, -1, -1)
        for s in order:
            cin = jnp.where(sub == s, c, cin)
            c = h_end[s:s + 1] + p_end[s:s + 1] * c
        return cin

    cin_f = carry_in(hf_end, pf_end, 0)
    cin_b = carry_in(hb_end, pb_end, 1)

    def out_body(it, carry):
        rows = pl.ds(pl.multiple_of(it * tc, tc), tc)
        z = z_ref[0, rows]
        h = (hf_ref[rows] + pf_ref[rows] * cin_f) + (hb_ref[rows] + pb_ref[rows] * cin_b)
        o_ref[0, rows] = (h * (z * _sigmoid(z))).astype(o_ref.dtype)
        return carry

    lax.fori_loop(0, n_it, out_body, 0)


def _lru(xz, cw, cb, w, b, lam):
    wc = LRU_WC
    nb = wc // LRU_BLOCK
    nj = LRU_W // wc
    blk = (1, SEG, NSEG, wc)
    seg_buf = pltpu.VMEM((SEG, NSEG, wc), F32)
    return pl.pallas_call(
        _lru_kernel,
        grid=(BATCH, nj),
        in_specs=[pl.BlockSpec(blk, lambda b_, j: (b_, 0, 0, j)),
                  pl.BlockSpec(blk, lambda b_, j: (b_, 0, 0, nj + j)),
                  pl.BlockSpec((CONV_W, wc), lambda b_, j: (0, j)),
                  pl.BlockSpec((1, wc), lambda b_, j: (0, j)),
                  pl.BlockSpec((2, nb, LRU_BLOCK, 2 * LRU_BLOCK), lambda b_, j: (0, j, 0, 0)),
                  pl.BlockSpec((2, nb, 1, 2 * LRU_BLOCK), lambda b_, j: (0, j, 0, 0)),
                  pl.BlockSpec((2, wc), lambda b_, j: (0, j))],
        out_specs=pl.BlockSpec(blk, lambda b_, j: (b_, 0, 0, j)),
        out_shape=jax.ShapeDtypeStruct((BATCH, SEG, NSEG, LRU_W), F32),
        scratch_shapes=[pltpu.VMEM((SEG + 3, NSEG, wc), F32),
                        seg_buf, seg_buf, seg_buf, seg_buf,
                        pltpu.VMEM((2, nb, 2 * LRU_BLOCK, 2 * LRU_BLOCK), BF16),
                        pltpu.VMEM((2, 2, LRU_TC, NSEG, wc), F32),
                        pltpu.VMEM((2, 2, LRU_TC, NSEG, wc), F32)],
        compiler_params=_cparams(2),
    )(xz, xz, cw, cb, w, b, lam)


def _branch_a_kernel(hf_ref, hb_ref, o_ref, za_ref, hg_ref, ga_ref, w_ref, out_ref, ya_ref):
    tm = ya_ref.shape[0]
    acc = None
    for hh in range(HEADS):
        cols = slice(hh * V_DIM, (hh + 1) * V_DIM)
        hg = hg_ref[:, cols]
        for r in range(tm // NORM_ROWS):
            rows = slice(r * NORM_ROWS, (r + 1) * NORM_ROWS)
            hs = hf_ref[0, rows, cols].astype(F32) + hb_ref[0, rows, cols].astype(F32)
            ms = jnp.mean(hs * hs, axis=-1, keepdims=True)
            hn = hs * lax.rsqrt(ms + NORM_EPS) * hg
            o = o_ref[0, rows, cols].astype(F32)
            z = za_ref[0, rows, cols].astype(F32)
            gate = (0.25 * z) * ((1.0 + jnp.tanh(0.5 * o)) * (1.0 + jnp.tanh(0.5 * z)))
            ya_ref[rows, cols] = (hn * gate).astype(BF16)
        part = jnp.dot(ya_ref[:, cols], w_ref[0, cols, :], preferred_element_type=F32)
        acc = part if acc is None else acc + part
    out_ref[0] = (_sigmoid(ga_ref[0].astype(F32)) * acc).astype(out_ref.dtype)


def _branch_a(hf, hb, head, gg, hg, w, layer, tm):
    wide = (1, tm, M_WIDTH)
    return pl.pallas_call(
        _branch_a_kernel,
        grid=(BATCH, SEQ // tm),
        in_specs=[pl.BlockSpec(wide, lambda b, i: (b, i, 0)),
                  pl.BlockSpec(wide, lambda b, i: (b, i, 0)),
                  pl.BlockSpec(wide, lambda b, i: (b, i, HEAD_O // M_WIDTH)),
                  pl.BlockSpec(wide, lambda b, i: (b, i, HEAD_ZA // M_WIDTH)),
                  pl.BlockSpec((1, M_WIDTH), lambda b, i: (0, 0)),
                  pl.BlockSpec((1, tm, D_MODEL), lambda b, i: (b, i, 0)),
                  pl.BlockSpec((1, M_WIDTH, D_MODEL), lambda b, i: (layer, 0, 0))],
        out_specs=pl.BlockSpec((1, tm, D_MODEL), lambda b, i: (b, i, 0)),
        out_shape=jax.ShapeDtypeStruct((BATCH, SEQ, D_MODEL), BF16),
        scratch_shapes=[pltpu.VMEM((tm, M_WIDTH), BF16)],
        compiler_params=_cparams(2),
    )(hf, hb, head, head, hg, gg, w)


def _branch_b_kernel(yb_ref, gb_ref, pa_ref, w_ref, out_ref, y16_ref):
    rows = NSEG * SEG_TILE
    step = 512
    for c in range(LRU_W // step):
        cols = slice(c * step, (c + 1) * step)
        y = jnp.swapaxes(yb_ref[0, :, :, cols], 0, 1)
        y16_ref[:, cols] = y.reshape(rows, step).astype(BF16)
    acc = jnp.dot(y16_ref[...], w_ref[0], preferred_element_type=F32)
    pa = pa_ref[0].reshape(rows, D_MODEL).astype(F32)
    gb = gb_ref[0].reshape(rows, D_MODEL).astype(F32)
    out_ref[0] = (pa + _sigmoid(gb) * acc).astype(out_ref.dtype).reshape(NSEG, SEG_TILE, D_MODEL)


def _branch_b(yb, gg, part_a, w, layer):
    seg4 = lambda arr: arr.reshape(BATCH, NSEG, SEG, arr.shape[-1])
    tile = (1, NSEG, SEG_TILE, D_MODEL)
    merged = pl.pallas_call(
        _branch_b_kernel,
        grid=(BATCH, SEG // SEG_TILE),
        in_specs=[pl.BlockSpec((1, SEG_TILE, NSEG, LRU_W), lambda b, i: (b, i, 0, 0)),
                  pl.BlockSpec(tile, lambda b, i: (b, 0, i, 1)),
                  pl.BlockSpec(tile, lambda b, i: (b, 0, i, 0)),
                  pl.BlockSpec((1, LRU_W, D_MODEL), lambda b, i: (layer, 0, 0))],
        out_specs=pl.BlockSpec(tile, lambda b, i: (b, 0, i, 0)),
        out_shape=jax.ShapeDtypeStruct((BATCH, NSEG, SEG, D_MODEL), BF16),
        scratch_shapes=[pltpu.VMEM((NSEG * SEG_TILE, LRU_W), BF16)],
        compiler_params=_cparams(2),
    )(yb, seg4(gg), seg4(part_a), w)
    return merged.reshape(BATCH, SEQ, D_MODEL)


def _out_kernel(m_ref, x_ref, w_ref, fg_ref, o_ref, *, final_norm):
    y = x_ref[0] + jnp.dot(m_ref[0], w_ref[0], preferred_element_type=F32)
    if final_norm:
        ms = jnp.mean(y * y, axis=-1, keepdims=True)
        y = y * lax.rsqrt(ms + NORM_EPS) * fg_ref[...]
    o_ref[0] = y


def _out_proj(merged, x, w, layer, fg, final_norm, tm):
    row = lambda b, i: (b, i, 0)
    return pl.pallas_call(
        functools.partial(_out_kernel, final_norm=final_norm),
        grid=(BATCH, SEQ // tm),
        in_specs=[pl.BlockSpec((1, tm, D_MODEL), row),
                  pl.BlockSpec((1, tm, D_MODEL), row),
                  pl.BlockSpec((1, D_MODEL, D_MODEL), lambda b, i: (layer, 0, 0)),
                  pl.BlockSpec((1, D_MODEL), lambda b, i: (0, 0))],
        out_specs=pl.BlockSpec((1, tm, D_MODEL), row),
        out_shape=jax.ShapeDtypeStruct((BATCH, SEQ, D_MODEL), F32),
        compiler_params=_cparams(2),
    )(merged, x, w, fg)


def kernel(x, norm_g, w_in, b_if, head_g, conv_w, conv_b, w_rg, b_rg, lru_lambda,
           w_branch_a, w_branch_b, w_out, final_g):
    w_t = jnp.swapaxes(w_in, 1, 2)
    bg = jnp.pad(b_if, ((0, 0), (0, 128 - N_GATE)))
    n_blk = LRU_W // LRU_BLOCK
    w_gate = jnp.transpose(w_rg, (0, 1, 3, 4, 2, 5)).reshape(
        DEPTH, 2, n_blk, LRU_BLOCK, 2 * LRU_BLOCK).astype(BF16)
    b_gate = jnp.transpose(b_rg.reshape(DEPTH, 2, 2, n_blk, LRU_BLOCK), (0, 1, 3, 2, 4)).reshape(
        DEPTH, 2, n_blk, 1, 2 * LRU_BLOCK)
    w_a16 = w_branch_a.astype(BF16)
    w_b16 = w_branch_b.astype(BF16)
    w_o16 = w_out.astype(BF16)

    head_col = lambda j: jnp.where(j < 4, j + 4, jnp.where(j < 6, j - 2, 0))
    xz_cols = 2 * LRU_W // MM_TN

    h = x
    for l in range(DEPTH):
        hn, kt, grow, gcol = _kgate(h, norm_g[l][None, :], w_t, l, b_if[l][:, None],
                                    bg[l][None, :], tm=1024)
        head = _proj(hn, w_t, l, 0, head_col, N_HEAD // MM_TN, BF16)
        gg = _proj(hn, w_t, l, _TAIL0, lambda j: xz_cols + j, 2 * D_MODEL // MM_TN, BF16)
        xz = _proj(hn, w_t, l, _TAIL0, lambda j: j, xz_cols, F32, segmented=True)

        hf, hb = _mlstm(head, kt, grow, gcol)
        yb = _lru(xz, conv_w[l], conv_b[l][None, :], w_gate[l], b_gate[l], lru_lambda[l])

        part_a = _branch_a(hf, hb, head, gg, head_g[l][None, :], w_a16, l, tm=512)
        merged = _branch_b(yb, gg, part_a, w_b16, l)
        h = _out_proj(merged, h, w_o16, l, final_g[None, :], l == DEPTH - 1, tm=512)
    return h
```

```python
import functools

import jax
import jax.numpy as jnp
import numpy as np
from jax import lax
from jax.experimental import pallas as pl
from jax.experimental.pallas import tpu as pltpu

F32 = jnp.float32
BF16 = jnp.bfloat16

D_MODEL = 2048
BATCH = 4
SEQ = 4096
DEPTH = 2
HEADS = 4
QK_DIM = 256
V_DIM = 512
QK_WIDTH = HEADS * QK_DIM
M_WIDTH = HEADS * V_DIM
N_GATE = 4 * HEADS
LRU_W = D_MODEL
LRU_BLOCK = 128
LRU_C = 8.0
CONV_W = 4
NORM_EPS = 1e-6
QK_SCALE = QK_DIM ** -0.5

_K0, _V0 = 1024, 2048
_G0 = 8192
_TAIL0 = _G0 + N_GATE

HEAD_O, HEAD_ZA, HEAD_V, HEAD_Q = 0, 2048, 4096, 6144
N_HEAD = 7168

CHUNK = 256
N_CHUNK = SEQ // CHUNK
V_AUG = V_DIM + 128
MLSTM_HPS = 4

NSEG = 8
SEG = SEQ // NSEG
LRU_WC = 256
LRU_TC = 32

SEG_TILE = 64
PROJ_SEG_TILE = 128
KG_ROWS = QK_WIDTH + 16 * HEADS

NORM_ROWS = 128
MM_TN = 1024
PROJ_TM = 2048
VMEM_LIMIT = 56 * 1024 * 1024


def _cparams(n_axes):
    return pltpu.CompilerParams(dimension_semantics=("arbitrary",) * n_axes,
                                vmem_limit_bytes=VMEM_LIMIT)


def _sigmoid(x):
    return 0.5 * jnp.tanh(0.5 * x) + 0.5


def _log_sigmoid(x):
    return jnp.minimum(x, 0.0) - jnp.log1p(jnp.exp(-jnp.abs(x)))


def _softplus(x):
    return jnp.maximum(x, 0.0) + jnp.log1p(jnp.exp(-jnp.abs(x)))


def _chunk_cumsum(x, axis, reverse):
    n = x.shape[axis]
    idx = lax.broadcasted_iota(jnp.int32, x.shape, axis) % CHUNK
    d = 1
    while d < CHUNK:
        if reverse:
            x = x + jnp.where(idx < CHUNK - d, pltpu.roll(x, n - d, axis), 0.0)
        else:
            x = x + jnp.where(idx >= d, pltpu.roll(x, d, axis), 0.0)
        d *= 2
    return x


def _kgate_kernel(x_ref, g_ref, wk_ref, wg_ref, bcol_ref, brow_ref,
                  hn_ref, kt_ref, grow_ref, gcol_ref, wkt_ref, wg16_ref):
    @pl.when((pl.program_id(0) == 0) & (pl.program_id(1) == 0))
    def _():
        step = 256
        for c in range(QK_WIDTH // step):
            wkt_ref[c * step:(c + 1) * step, :] = wk_ref[0, c * step:(c + 1) * step, :].astype(BF16)
        wg16_ref[...] = jnp.zeros_like(wg16_ref)
        wg16_ref[:N_GATE, :] = wg_ref[0].astype(BF16)
        wkt_ref[QK_WIDTH:, :] = wg16_ref[...]

    g = g_ref[...]

    def norm_body(r, carry):
        rows = pl.ds(pl.multiple_of(r * NORM_ROWS, NORM_ROWS), NORM_ROWS)
        x = x_ref[0, rows, :]
        ms = jnp.mean(x * x, axis=-1, keepdims=True)
        hn_ref[0, rows, :] = (x * lax.rsqrt(ms + NORM_EPS) * g).astype(BF16)
        return carry

    lax.fori_loop(0, hn_ref.shape[1] // NORM_ROWS, norm_body, 0)
    hn = hn_ref[0]
    nt = (((1,), (1,)), ((), ()))
    kg = lax.dot_general(wkt_ref[...], hn, nt, preferred_element_type=F32)
    kt_ref[...] = (kg[:QK_WIDTH] * QK_SCALE).astype(BF16)

    wg = wg16_ref[...]
    xc = lax.dot_general(hn, wg, nt, preferred_element_type=F32) + brow_ref[...]
    lfc = _log_sigmoid(xc)
    lane = lax.broadcasted_iota(jnp.int32, lfc.shape, 1)
    gcol_ref[...] = jnp.where(lane < 3 * HEADS, _chunk_cumsum(lfc, 0, False),
                              _chunk_cumsum(lfc, 0, True))

    xr = kg[QK_WIDTH:QK_WIDTH + N_GATE] + bcol_ref[...]
    lf = _log_sigmoid(xr[2 * HEADS:])
    r8 = lax.broadcasted_iota(jnp.int32, lf.shape, 0)
    cum = jnp.where(r8 < HEADS, _chunk_cumsum(lf, 1, False), _chunk_cumsum(lf, 1, True))
    grow_ref[:2 * HEADS, :] = xr[:2 * HEADS] - cum
    grow_ref[2 * HEADS:, :] = cum


def _kgate(x, g, w_t, layer, bcol, brow, tm):
    nt = SEQ // tm
    tok = BATCH * SEQ
    full = lambda shape: pl.BlockSpec(shape, lambda b, i: (0,) * len(shape))
    return pl.pallas_call(
        _kgate_kernel,
        grid=(BATCH, nt),
        in_specs=[pl.BlockSpec((1, tm, D_MODEL), lambda b, i: (b, i, 0)),
                  full((1, D_MODEL)),
                  pl.BlockSpec((1, QK_WIDTH, D_MODEL), lambda b, i: (layer, _K0 // QK_WIDTH, 0)),
                  pl.BlockSpec((1, N_GATE, D_MODEL), lambda b, i: (layer, _G0 // N_GATE, 0)),
                  full((N_GATE, 1)), full((1, 128))],
        out_specs=[pl.BlockSpec((1, tm, D_MODEL), lambda b, i: (b, i, 0)),
                   pl.BlockSpec((QK_WIDTH, tm), lambda b, i: (0, b * nt + i)),
                   pl.BlockSpec((N_GATE, tm), lambda b, i: (0, b * nt + i)),
                   pl.BlockSpec((tm, 128), lambda b, i: (b * nt + i, 0))],
        out_shape=[jax.ShapeDtypeStruct((BATCH, SEQ, D_MODEL), BF16),
                   jax.ShapeDtypeStruct((QK_WIDTH, tok), BF16),
                   jax.ShapeDtypeStruct((N_GATE, tok), F32),
                   jax.ShapeDtypeStruct((tok, 128), F32)],
        scratch_shapes=[pltpu.VMEM((QK_WIDTH + 128, D_MODEL), BF16),
                        pltpu.VMEM((128, D_MODEL), BF16)],
        compiler_params=_cparams(2),
    )(x, g, w_t, w_t, bcol, brow)


W_CAST_ROWS = 128


def _proj_kernel(*refs, shift, segmented):
    if shift:
        a_ref, w_ref, w2_ref, o_ref, w16_ref = refs
    else:
        a_ref, w_ref, o_ref, w16_ref = refs

    @pl.when((pl.program_id(1) == 0) & (pl.program_id(2) == 0))
    def _():
        for r in range(MM_TN // W_CAST_ROWS):
            lo = shift + r * W_CAST_ROWS
            hi = lo + W_CAST_ROWS
            if hi <= MM_TN:
                w = w_ref[0, lo:hi, :]
            else:
                w = jnp.concatenate([w_ref[0, lo:MM_TN, :], w2_ref[0, :hi - MM_TN, :]], axis=0)
            w16_ref[r * W_CAST_ROWS:(r + 1) * W_CAST_ROWS, :] = w.astype(BF16)

    nt = (((1,), (1,)), ((), ()))
    if segmented:
        a = a_ref[0].reshape(NSEG * PROJ_SEG_TILE, D_MODEL)
        r = lax.dot_general(a, w16_ref[...], nt, preferred_element_type=F32)
        o_ref[0] = jnp.swapaxes(r.reshape(NSEG, PROJ_SEG_TILE, MM_TN), 0, 1).astype(o_ref.dtype)
    else:
        o_ref[0] = lax.dot_general(a_ref[0], w16_ref[...], nt,
                                   preferred_element_type=F32).astype(o_ref.dtype)


def _proj(hn, w_t, layer, col0, w_col, n_col, out_dtype, segmented=False):
    shift = col0 % MM_TN
    base = col0 // MM_TN
    assert shift % 8 == 0 and MM_TN % max(shift, 1) == 0
    w_specs = [pl.BlockSpec((1, MM_TN, D_MODEL), lambda j, b, i: (layer, base + w_col(j), 0))]
    operands = [w_t]
    if shift:
        per = MM_TN // shift
        w_specs.append(pl.BlockSpec((1, shift, D_MODEL),
                                    lambda j, b, i: (layer, (base + w_col(j) + 1) * per, 0)))
        operands.append(w_t)
    if segmented:
        a = hn.reshape(BATCH, NSEG, SEG, D_MODEL)
        grid = (n_col, BATCH, SEG // PROJ_SEG_TILE)
        a_spec = pl.BlockSpec((1, NSEG, PROJ_SEG_TILE, D_MODEL), lambda j, b, i: (b, 0, i, 0))
        out_spec = pl.BlockSpec((1, PROJ_SEG_TILE, NSEG, MM_TN), lambda j, b, i: (b, i, 0, j))
        out_shape = jax.ShapeDtypeStruct((BATCH, SEG, NSEG, n_col * MM_TN), out_dtype)
    else:
        a = hn
        grid = (n_col, BATCH, SEQ // PROJ_TM)
        a_spec = pl.BlockSpec((1, PROJ_TM, D_MODEL), lambda j, b, i: (b, i, 0))
        out_spec = pl.BlockSpec((1, PROJ_TM, MM_TN), lambda j, b, i: (b, i, j))
        out_shape = jax.ShapeDtypeStruct((BATCH, SEQ, n_col * MM_TN), out_dtype)
    return pl.pallas_call(
        functools.partial(_proj_kernel, shift=shift, segmented=segmented),
        grid=grid,
        in_specs=[a_spec] + w_specs,
        out_specs=out_spec,
        out_shape=out_shape,
        scratch_shapes=[pltpu.VMEM((MM_TN, D_MODEL), BF16)],
        compiler_params=_cparams(3),
    )(a, *operands)


def _mlstm_kernel(qf_ref, ktf_ref, vf_ref, rowf_ref, colf_ref,
                  qb_ref, ktb_ref, vb_ref, rowb_ref, colb_ref,
                  hf_ref, hb_ref, ct_ref, m_ref):
    @pl.when(pl.program_id(2) == 0)
    def _():
        ct_ref[...] = jnp.zeros_like(ct_ref)
        m_ref[...] = jnp.zeros_like(m_ref)

    L = CHUNK
    jj = lax.broadcasted_iota(jnp.int32, (L, L), 0)
    ss = lax.broadcasted_iota(jnp.int32, (L, L), 1)
    lane = lax.broadcasted_iota(jnp.int32, (L, 128), 1)
    ones_blk = jnp.where(lane == 0, 1.0, 0.0).astype(BF16)

    dirs = ((qf_ref, ktf_ref, vf_ref, rowf_ref, colf_ref, hf_ref),
            (qb_ref, ktb_ref, vb_ref, rowb_ref, colb_ref, hb_ref))
    chains = [(hh, d) for hh in range(MLSTM_HPS) for d in range(2)]
    for hh, d in chains:
        q_ref, kt_ref, v_ref, row_ref, col_ref, out_ref = dirs[d]
        head = pl.program_id(1) * MLSTM_HPS + hh
        st = d * MLSTM_HPS + hh
        q = q_ref[0, :, hh * QK_DIM:(hh + 1) * QK_DIM]
        kt = kt_ref[hh * QK_DIM:(hh + 1) * QK_DIM, :]
        v_aug = jnp.concatenate([v_ref[0, :, hh * V_DIM:(hh + 1) * V_DIM], ones_blk],
                                axis=1)
        a_row = row_ref[pl.ds(HEADS * d + head, 1), :]
        b_row = row_ref[pl.ds(2 * HEADS + HEADS * d + head, 1), :]
        b_col = jnp.sum(jnp.where(lane == 2 * HEADS + HEADS * d + head, col_ref[...], 0.0),
                        axis=1, keepdims=True)
        m_prev = m_ref[st, 0:1, 0:1]

        causal = (ss <= jj) if d == 0 else (ss >= jj)
        d_log = jnp.where(causal, b_col + a_row, -jnp.inf)
        inter = b_col + m_prev
        m_row = jnp.maximum(inter, jnp.max(d_log, axis=1, keepdims=True))
        w_intra = jnp.exp(d_log - m_row)
        w_inter = jnp.exp(inter - m_row)

        s = jnp.dot(q, kt, preferred_element_type=F32)
        p = (s * w_intra).astype(BF16)
        ct = ct_ref[st]
        nd = (jnp.dot(p, v_aug, preferred_element_type=F32)
              + w_inter * jnp.dot(q, ct.astype(BF16), preferred_element_type=F32))
        num = nd[:, :V_DIM]
        den = nd[:, V_DIM:V_DIM + 1]
        inv = 1.0 / jnp.maximum(jnp.abs(den), jnp.exp(-m_row))
        out_ref[0, :, hh * V_DIM:(hh + 1) * V_DIM] = (num * inv).astype(out_ref.dtype)

        g_tot = b_row[:, L - 1:L] if d == 0 else b_row[:, 0:1]
        w_log = g_tot + a_row
        m_new = jnp.maximum(g_tot + m_prev, jnp.max(w_log, axis=1, keepdims=True))
        w_k = jnp.exp(w_log - m_new)
        decay = jnp.exp(g_tot + m_prev - m_new)
        ktw = (kt.astype(F32) * w_k).astype(BF16)
        ct_ref[st] = decay * ct + jnp.dot(ktw, v_aug, preferred_element_type=F32)
        m_ref[st] = jnp.broadcast_to(m_new, (8, 128))


def _mlstm(head, kt, grow, gcol):
    L, nc, hps = CHUNK, N_CHUNK, MLSTM_HPS
    qw, vw = hps * QK_DIM, hps * V_DIM
    rev = lambda c: nc - 1 - c

    def specs(cmap):
        return [
            pl.BlockSpec((1, L, qw), lambda b, h, c: (b, cmap(c), HEAD_Q // qw + h)),
            pl.BlockSpec((qw, L), lambda b, h, c: (h, b * nc + cmap(c))),
            pl.BlockSpec((1, L, vw), lambda b, h, c: (b, cmap(c), HEAD_V // vw + h)),
            pl.BlockSpec((N_GATE, L), lambda b, h, c: (0, b * nc + cmap(c))),
            pl.BlockSpec((L, 128), lambda b, h, c: (b * nc + cmap(c), 0)),
        ]

    fwd = lambda c: c
    out_sds = jax.ShapeDtypeStruct((BATCH, SEQ, M_WIDTH), BF16)
    return pl.pallas_call(
        _mlstm_kernel,
        grid=(BATCH, HEADS // hps, nc),
        in_specs=specs(fwd) + specs(rev),
        out_specs=[pl.BlockSpec((1, L, vw), lambda b, h, c: (b, c, h)),
                   pl.BlockSpec((1, L, vw), lambda b, h, c: (b, rev(c), h))],
        out_shape=[out_sds, out_sds],
        scratch_shapes=[pltpu.VMEM((2 * hps, QK_DIM, V_AUG), F32),
                        pltpu.VMEM((2 * hps, 8, 128), F32)],
        compiler_params=_cparams(3),
    )(head, kt, head, grow, gcol, head, kt, head, grow, gcol)


def _lru_kernel(x_ref, z_ref, cw_ref, cb_ref, w_ref, b_ref, lam_ref, o_ref,
                xs_ref, hf_ref, pf_ref, hb_ref, pb_ref, wh_ref, a_ref, u_ref):
    wc = x_ref.shape[-1]
    nb = wc // LRU_BLOCK
    tc = LRU_TC
    n_it = SEG // tc
    sub = lax.broadcasted_iota(jnp.int32, (NSEG, wc), 0)

    def copy_body(it, carry):
        t0 = pl.multiple_of(it * tc, tc)
        xs_ref[pl.ds(t0 + 2, tc)] = x_ref[0, pl.ds(t0, tc)]
        return carry

    lax.fori_loop(0, n_it, copy_body, 0)
    for r in range(2):
        prev = pltpu.roll(x_ref[0, SEG - 2 + r], 1, 0)
        xs_ref[r] = jnp.where(sub >= 1, prev, 0.0)
    nxt = pltpu.roll(x_ref[0, 0], NSEG - 1, 0)
    xs_ref[SEG + 2] = jnp.where(sub <= NSEG - 2, nxt, 0.0)

    cw = [cw_ref[t:t + 1, :][None] for t in range(CONV_W)]
    cb = cb_ref[...][None]

    def conv_body(it, carry):
        t0 = pl.multiple_of(it * tc, tc)
        acc = xs_ref[pl.ds(t0, tc)] * cw[0]
        for t in range(1, CONV_W):
            acc = acc + xs_ref[pl.ds(t0 + t, tc)] * cw[t]
        xs_ref[pl.ds(t0, tc)] = cb + acc
        return carry

    lax.fori_loop(0, n_it, conv_body, 0)

    wh_ref[...] = jnp.zeros_like(wh_ref)
    wh_ref[:, :, :LRU_BLOCK, :] = (0.5 * w_ref[...].astype(F32)).astype(BF16)
    bh = 0.5 * b_ref[...]
    b_hi = bh.astype(BF16)
    b_lo = (bh - b_hi.astype(F32)).astype(BF16)
    wh_ref[:, :, LRU_BLOCK:LRU_BLOCK + 1, :] = b_hi
    wh_ref[:, :, LRU_BLOCK + 1:LRU_BLOCK + 2, :] = b_lo
    lane_blk = lax.broadcasted_iota(jnp.int32, (tc * NSEG, LRU_BLOCK), 1)
    ones_blk = jnp.where(lane_blk < 2, 1.0, 0.0).astype(BF16)
    hcs = (-0.5 * LRU_C) * _softplus(-lam_ref[...])

    def conv(t0):
        return xs_ref[pl.ds(t0, tc)]

    def gates(xc, d):
        x2 = xc.reshape(tc * NSEG, wc)
        x16 = x2.astype(BF16)
        hx = 0.5 * x2
        a_parts, u_parts = [], []
        for j in range(nb):
            cols = slice(j * LRU_BLOCK, (j + 1) * LRU_BLOCK)
            lhs = jnp.concatenate([x16[:, cols], ones_blk], axis=1)
            pre = jnp.dot(lhs, wh_ref[d, j], preferred_element_type=F32)
            t_r = jnp.tanh(pre[:, :LRU_BLOCK])
            t_i = jnp.tanh(pre[:, LRU_BLOCK:])
            h = hcs[d:d + 1, cols]
            log_a = h * t_r + h
            a_parts.append(jnp.exp(log_a))
            th = jnp.tanh(log_a)
            p = -2.0 * th
            q = 1.0 - th
            coef = jnp.where(p > 0.0, p * lax.rsqrt(p * q), 0.0)
            u_parts.append(coef * ((t_i + 1.0) * hx[:, cols]))
        a = jnp.concatenate(a_parts, axis=1).reshape(tc, NSEG, wc)
        u = jnp.concatenate(u_parts, axis=1).reshape(tc, NSEG, wc)
        return a, u

    def chunk_of(it, d):
        return it if d == 0 else n_it - 1 - it

    def gates_to(slot, it, d):
        t0 = pl.multiple_of(chunk_of(it, d) * tc, tc)
        a, u = gates(conv(t0), d)
        a_ref[d, slot] = a
        u_ref[d, slot] = u

    def half_step(it, carry, slot):
        cur = [(a_ref[d, slot], u_ref[d, slot]) for d in range(2)]
        nxt = jnp.minimum(it + 1, n_it - 1)
        for d in range(2):
            gates_to(1 - slot, nxt, d)
        out = []
        for d, (h_ref, p_ref) in enumerate(((hf_ref, pf_ref), (hb_ref, pb_ref))):
            h, p = carry[2 * d], carry[2 * d + 1]
            a, u = cur[d]
            t0 = pl.multiple_of(chunk_of(it, d) * tc, tc)
            hs, ps = [None] * tc, [None] * tc
            order = range(tc) if d == 0 else range(tc - 1, -1, -1)
            for k in order:
                h = a[k] * h + u[k]
                p = a[k] * p
                hs[k], ps[k] = h, p
            h_ref[pl.ds(t0, tc)] = jnp.stack(hs)
            p_ref[pl.ds(t0, tc)] = jnp.stack(ps)
            out += [h, p]
        return tuple(out)

    def scan_body(i2, carry):
        carry = half_step(2 * i2, carry, 0)
        return half_step(2 * i2 + 1, carry, 1)

    for d in range(2):
        gates_to(0, 0, d)
    zeros = jnp.zeros((NSEG, wc), F32)
    ones = jnp.ones((NSEG, wc), F32)
    hf_end, pf_end, hb_end, pb_end = lax.fori_loop(0, n_it // 2, scan_body,
                                                   (zeros, ones, zeros, ones))

    def carry_in(h_end, p_end, d):
        cin = jnp.zeros((NSEG, wc), F32)
        c = jnp.zeros((1, wc), F32)
        order = range(NSEG) if d == 0 else range(NSEG - 1, -1, -1)
        for s in order:
            cin = jnp.where(sub == s, c, cin)
            c = h_end[s:s + 1] + p_end[s:s + 1] * c
        return cin

    cin_f = carry_in(hf_end, pf_end, 0)
    cin_b = carry_in(hb_end, pb_end, 1)

    def out_body(it, carry):
        rows = pl.ds(pl.multiple_of(it * tc, tc), tc)
        z = z_ref[0, rows]
        h = (hf_ref[rows] + pf_ref[rows] * cin_f) + (hb_ref[rows] + pb_ref[rows] * cin_b)
        o_ref[0, rows] = (h * (z * _sigmoid(z))).astype(o_ref.dtype)
        return carry

    lax.fori_loop(0, n_it, out_body, 0)


def _lru(xz, cw, cb, w, b, lam):
    wc = LRU_WC
    nb = wc // LRU_BLOCK
    nj = LRU_W // wc
    blk = (1, SEG, NSEG, wc)
    seg_buf = pltpu.VMEM((SEG, NSEG, wc), F32)
    return pl.pallas_call(
        _lru_kernel,
        grid=(BATCH, nj),
        in_specs=[pl.BlockSpec(blk, lambda b_, j: (b_, 0, 0, j)),
                  pl.BlockSpec(blk, lambda b_, j: (b_, 0, 0, nj + j)),
                  pl.BlockSpec((CONV_W, wc), lambda b_, j: (0, j)),
                  pl.BlockSpec((1, wc), lambda b_, j: (0, j)),
                  pl.BlockSpec((2, nb, LRU_BLOCK, 2 * LRU_BLOCK), lambda b_, j: (0, j, 0, 0)),
                  pl.BlockSpec((2, nb, 1, 2 * LRU_BLOCK), lambda b_, j: (0, j, 0, 0)),
                  pl.BlockSpec((2, wc), lambda b_, j: (0, j))],
        out_specs=pl.BlockSpec(blk, lambda b_, j: (b_, 0, 0, j)),
        out_shape=jax.ShapeDtypeStruct((BATCH, SEG, NSEG, LRU_W), F32),
        scratch_shapes=[pltpu.VMEM((SEG + 3, NSEG, wc), F32),
                        seg_buf, seg_buf, seg_buf, seg_buf,
                        pltpu.VMEM((2, nb, 2 * LRU_BLOCK, 2 * LRU_BLOCK), BF16),
                        pltpu.VMEM((2, 2, LRU_TC, NSEG, wc), F32),
                        pltpu.VMEM((2, 2, LRU_TC, NSEG, wc), F32)],
        compiler_params=_cparams(2),
    )(xz, xz, cw, cb, w, b, lam)


def _branch_a_kernel(hf_ref, hb_ref, o_ref, za_ref, hg_ref, ga_ref, w_ref, out_ref, ya_ref):
    tm = ya_ref.shape[0]
    acc = None
    for hh in range(HEADS):
        cols = slice(hh * V_DIM, (hh + 1) * V_DIM)
        hg = hg_ref[:, cols]
        for r in range(tm // NORM_ROWS):
            rows = slice(r * NORM_ROWS, (r + 1) * NORM_ROWS)
            hs = hf_ref[0, rows, cols].astype(F32) + hb_ref[0, rows, cols].astype(F32)
            ms = jnp.mean(hs * hs, axis=-1, keepdims=True)
            hn = hs * lax.rsqrt(ms + NORM_EPS) * hg
            o = o_ref[0, rows, cols]
            z = za_ref[0, rows, cols]
            gate = (0.25 * z) * ((1.0 + jnp.tanh(0.5 * o)) * (1.0 + jnp.tanh(0.5 * z)))
            ya_ref[rows, cols] = (hn * gate.astype(F32)).astype(BF16)
        part = jnp.dot(ya_ref[:, cols], w_ref[0, cols, :], preferred_element_type=F32)
        acc = part if acc is None else acc + part
    out_ref[0] = (_sigmoid(ga_ref[0]).astype(F32) * acc).astype(out_ref.dtype)


def _branch_a(hf, hb, head, gg, hg, w, layer, tm):
    wide = (1, tm, M_WIDTH)
    return pl.pallas_call(
        _branch_a_kernel,
        grid=(BATCH, SEQ // tm),
        in_specs=[pl.BlockSpec(wide, lambda b, i: (b, i, 0)),
                  pl.BlockSpec(wide, lambda b, i: (b, i, 0)),
                  pl.BlockSpec(wide, lambda b, i: (b, i, HEAD_O // M_WIDTH)),
                  pl.BlockSpec(wide, lambda b, i: (b, i, HEAD_ZA // M_WIDTH)),
                  pl.BlockSpec((1, M_WIDTH), lambda b, i: (0, 0)),
                  pl.BlockSpec((1, tm, D_MODEL), lambda b, i: (b, i, 0)),
                  pl.BlockSpec((1, M_WIDTH, D_MODEL), lambda b, i: (layer, 0, 0))],
        out_specs=pl.BlockSpec((1, tm, D_MODEL), lambda b, i: (b, i, 0)),
        out_shape=jax.ShapeDtypeStruct((BATCH, SEQ, D_MODEL), BF16),
        scratch_shapes=[pltpu.VMEM((tm, M_WIDTH), BF16)],
        compiler_params=_cparams(2),
    )(hf, hb, head, head, hg, gg, w)


def _branch_b_kernel(yb_ref, gb_ref, pa_ref, w_ref, out_ref, y16_ref):
    rows = NSEG * SEG_TILE
    step = 512
    for c in range(LRU_W // step):
        cols = slice(c * step, (c + 1) * step)
        y = jnp.swapaxes(yb_ref[0, :, :, cols], 0, 1)
        y16_ref[:, cols] = y.reshape(rows, step).astype(BF16)
    acc = jnp.dot(y16_ref[...], w_ref[0], preferred_element_type=F32)
    pa = pa_ref[0].reshape(rows, D_MODEL).astype(F32)
    gb = _sigmoid(gb_ref[0].reshape(rows, D_MODEL)).astype(F32)
    out_ref[0] = (pa + gb * acc).astype(out_ref.dtype).reshape(NSEG, SEG_TILE, D_MODEL)


def _branch_b(yb, gg, part_a, w, layer):
    seg4 = lambda arr: arr.reshape(BATCH, NSEG, SEG, arr.shape[-1])
    tile = (1, NSEG, SEG_TILE, D_MODEL)
    merged = pl.pallas_call(
        _branch_b_kernel,
        grid=(BATCH, SEG // SEG_TILE),
        in_specs=[pl.BlockSpec((1, SEG_TILE, NSEG, LRU_W), lambda b, i: (b, i, 0, 0)),
                  pl.BlockSpec(tile, lambda b, i: (b, 0, i, 1)),
                  pl.BlockSpec(tile, lambda b, i: (b, 0, i, 0)),
                  pl.BlockSpec((1, LRU_W, D_MODEL), lambda b, i: (layer, 0, 0))],
        out_specs=pl.BlockSpec(tile, lambda b, i: (b, 0, i, 0)),
        out_shape=jax.ShapeDtypeStruct((BATCH, NSEG, SEG, D_MODEL), BF16),
        scratch_shapes=[pltpu.VMEM((NSEG * SEG_TILE, LRU_W), BF16)],
        compiler_params=_cparams(2),
    )(yb, seg4(gg), seg4(part_a), w)
    return merged.reshape(BATCH, SEQ, D_MODEL)


def _out_kernel(m_ref, x_ref, w_ref, fg_ref, o_ref, *, final_norm):
    y = x_ref[0] + jnp.dot(m_ref[0], w_ref[0], preferred_element_type=F32)
    if final_norm:
        ms = jnp.mean(y * y, axis=-1, keepdims=True)
        y = y * lax.rsqrt(ms + NORM_EPS) * fg_ref[...]
    o_ref[0] = y


def _out_proj(merged, x, w, layer, fg, final_norm, tm):
    row = lambda b, i: (b, i, 0)
    return pl.pallas_call(
        functools.partial(_out_kernel, final_norm=final_norm),
        grid=(BATCH, SEQ // tm),
        in_specs=[pl.BlockSpec((1, tm, D_MODEL), row),
                  pl.BlockSpec((1, tm, D_MODEL), row),
                  pl.BlockSpec((1, D_MODEL, D_MODEL), lambda b, i: (layer, 0, 0)),
                  pl.BlockSpec((1, D_MODEL), lambda b, i: (0, 0))],
        out_specs=pl.BlockSpec((1, tm, D_MODEL), row),
        out_shape=jax.ShapeDtypeStruct((BATCH, SEQ, D_MODEL), F32),
        compiler_params=_cparams(2),
    )(merged, x, w, fg)


def kernel(x, norm_g, w_in, b_if, head_g, conv_w, conv_b, w_rg, b_rg, lru_lambda,
           w_branch_a, w_branch_b, w_out, final_g):
    w_t = jnp.swapaxes(w_in, 1, 2)
    bg = jnp.pad(b_if, ((0, 0), (0, 128 - N_GATE)))
    n_blk = LRU_W // LRU_BLOCK
    w_gate = jnp.transpose(w_rg, (0, 1, 3, 4, 2, 5)).reshape(
        DEPTH, 2, n_blk, LRU_BLOCK, 2 * LRU_BLOCK).astype(BF16)
    b_gate = jnp.transpose(b_rg.reshape(DEPTH, 2, 2, n_blk, LRU_BLOCK), (0, 1, 3, 2, 4)).reshape(
        DEPTH, 2, n_blk, 1, 2 * LRU_BLOCK)
    w_a16 = w_branch_a.astype(BF16)
    w_b16 = w_branch_b.astype(BF16)
    w_o16 = w_out.astype(BF16)

    head_col = lambda j: jnp.where(j < 4, j + 4, jnp.where(j < 6, j - 2, 0))
    xz_cols = 2 * LRU_W // MM_TN

    h = x
    for l in range(DEPTH):
        hn, kt, grow, gcol = _kgate(h, norm_g[l][None, :], w_t, l, b_if[l][:, None],
                                    bg[l][None, :], tm=1024)
        head = _proj(hn, w_t, l, 0, head_col, N_HEAD // MM_TN, BF16)
        gg = _proj(hn, w_t, l, _TAIL0, lambda j: xz_cols + j, 2 * D_MODEL // MM_TN, BF16)
        xz = _proj(hn, w_t, l, _TAIL0, lambda j: j, xz_cols, F32, segmented=True)

        hf, hb = _mlstm(head, kt, grow, gcol)
        yb = _lru(xz, conv_w[l], conv_b[l][None, :], w_gate[l], b_gate[l], lru_lambda[l])

        part_a = _branch_a(hf, hb, head, gg, head_g[l][None, :], w_a16, l, tm=512)
        merged = _branch_b(yb, gg, part_a, w_b16, l)
        h = _out_proj(merged, h, w_o16, l, final_g[None, :], l == DEPTH - 1, tm=512)
    return h
```

```python
import functools

import jax
import jax.numpy as jnp
import numpy as np
from jax import lax
from jax.experimental import pallas as pl
from jax.experimental.pallas import tpu as pltpu

F32 = jnp.float32
BF16 = jnp.bfloat16

D_MODEL = 2048
BATCH = 4
SEQ = 4096
DEPTH = 2
HEADS = 4
QK_DIM = 256
V_DIM = 512
QK_WIDTH = HEADS * QK_DIM
M_WIDTH = HEADS * V_DIM
N_GATE = 4 * HEADS
LRU_W = D_MODEL
LRU_BLOCK = 128
LRU_C = 8.0
CONV_W = 4
NORM_EPS = 1e-6
QK_SCALE = QK_DIM ** -0.5

_K0, _V0 = 1024, 2048
_G0 = 8192
_TAIL0 = _G0 + N_GATE

HEAD_O, HEAD_ZA, HEAD_V, HEAD_Q = 0, 2048, 4096, 6144
N_HEAD = 7168

CHUNK = 256
N_CHUNK = SEQ // CHUNK
V_AUG = V_DIM + 128
MLSTM_HPS = 4

NSEG = 8
SEG = SEQ // NSEG
LRU_WC = 256
LRU_TC = 32

SEG_TILE = 64
PROJ_SEG_TILE = 128
KG_ROWS = QK_WIDTH + 16 * HEADS

NORM_ROWS = 128
MM_TN = 1024
PROJ_TM = 2048
VMEM_LIMIT = 56 * 1024 * 1024


def _cparams(n_axes):
    return pltpu.CompilerParams(dimension_semantics=("arbitrary",) * n_axes,
                                vmem_limit_bytes=VMEM_LIMIT)


def _sigmoid(x):
    return 0.5 * jnp.tanh(0.5 * x) + 0.5


def _log_sigmoid(x):
    return jnp.minimum(x, 0.0) - jnp.log1p(jnp.exp(-jnp.abs(x)))


def _softplus(x):
    return jnp.maximum(x, 0.0) + jnp.log1p(jnp.exp(-jnp.abs(x)))


def _chunk_cumsum(x, axis, reverse):
    n = x.shape[axis]
    idx = lax.broadcasted_iota(jnp.int32, x.shape, axis) % CHUNK
    d = 1
    while d < CHUNK:
        if reverse:
            x = x + jnp.where(idx < CHUNK - d, pltpu.roll(x, n - d, axis), 0.0)
        else:
            x = x + jnp.where(idx >= d, pltpu.roll(x, d, axis), 0.0)
        d *= 2
    return x


def _kgate_kernel(x_ref, g_ref, wk_ref, wg_ref, bcol_ref, brow_ref,
                  hn_ref, kt_ref, grow_ref, gcol_ref, wkt_ref, wg16_ref):
    @pl.when((pl.program_id(0) == 0) & (pl.program_id(1) == 0))
    def _():
        step = 256
        for c in range(QK_WIDTH // step):
            wkt_ref[c * step:(c + 1) * step, :] = wk_ref[0, c * step:(c + 1) * step, :].astype(BF16)
        wg16_ref[...] = jnp.zeros_like(wg16_ref)
        wg16_ref[:N_GATE, :] = wg_ref[0].astype(BF16)
        wkt_ref[QK_WIDTH:, :] = wg16_ref[...]

    g = g_ref[...]

    def norm_body(r, carry):
        rows = pl.ds(pl.multiple_of(r * NORM_ROWS, NORM_ROWS), NORM_ROWS)
        x = x_ref[0, rows, :]
        ms = jnp.mean(x * x, axis=-1, keepdims=True)
        hn_ref[0, rows, :] = (x * lax.rsqrt(ms + NORM_EPS) * g).astype(BF16)
        return carry

    lax.fori_loop(0, hn_ref.shape[1] // NORM_ROWS, norm_body, 0)
    hn = hn_ref[0]
    nt = (((1,), (1,)), ((), ()))
    kg = lax.dot_general(wkt_ref[...], hn, nt, preferred_element_type=F32)
    kt_ref[...] = (kg[:QK_WIDTH] * QK_SCALE).astype(BF16)

    wg = wg16_ref[...]
    xc = lax.dot_general(hn, wg, nt, preferred_element_type=F32) + brow_ref[...]
    lfc = _log_sigmoid(xc)
    lane = lax.broadcasted_iota(jnp.int32, lfc.shape, 1)
    gcol_ref[...] = jnp.where(lane < 3 * HEADS, _chunk_cumsum(lfc, 0, False),
                              _chunk_cumsum(lfc, 0, True))

    xr = kg[QK_WIDTH:QK_WIDTH + N_GATE] + bcol_ref[...]
    lf = _log_sigmoid(xr[2 * HEADS:])
    r8 = lax.broadcasted_iota(jnp.int32, lf.shape, 0)
    cum = jnp.where(r8 < HEADS, _chunk_cumsum(lf, 1, False), _chunk_cumsum(lf, 1, True))
    grow_ref[:2 * HEADS, :] = xr[:2 * HEADS] - cum
    grow_ref[2 * HEADS:, :] = cum


def _kgate(x, g, w_t, layer, bcol, brow, tm):
    nt = SEQ // tm
    tok = BATCH * SEQ
    full = lambda shape: pl.BlockSpec(shape, lambda b, i: (0,) * len(shape))
    return pl.pallas_call(
        _kgate_kernel,
        grid=(BATCH, nt),
        in_specs=[pl.BlockSpec((1, tm, D_MODEL), lambda b, i: (b, i, 0)),
                  full((1, D_MODEL)),
                  pl.BlockSpec((1, QK_WIDTH, D_MODEL), lambda b, i: (layer, _K0 // QK_WIDTH, 0)),
                  pl.BlockSpec((1, N_GATE, D_MODEL), lambda b, i: (layer, _G0 // N_GATE, 0)),
                  full((N_GATE, 1)), full((1, 128))],
        out_specs=[pl.BlockSpec((1, tm, D_MODEL), lambda b, i: (b, i, 0)),
                   pl.BlockSpec((QK_WIDTH, tm), lambda b, i: (0, b * nt + i)),
                   pl.BlockSpec((N_GATE, tm), lambda b, i: (0, b * nt + i)),
                   pl.BlockSpec((tm, 128), lambda b, i: (b * nt + i, 0))],
        out_shape=[jax.ShapeDtypeStruct((BATCH, SEQ, D_MODEL), BF16),
                   jax.ShapeDtypeStruct((QK_WIDTH, tok), BF16),
                   jax.ShapeDtypeStruct((N_GATE, tok), F32),
                   jax.ShapeDtypeStruct((tok, 128), F32)],
        scratch_shapes=[pltpu.VMEM((QK_WIDTH + 128, D_MODEL), BF16),
                        pltpu.VMEM((128, D_MODEL), BF16)],
        compiler_params=_cparams(2),
    )(x, g, w_t, w_t, bcol, brow)


W_CAST_ROWS = 128


def _proj_kernel(*refs, shift, segmented):
    if shift:
        a_ref, w_ref, w2_ref, o_ref, w16_ref = refs
    else:
        a_ref, w_ref, o_ref, w16_ref = refs

    @pl.when((pl.program_id(1) == 0) & (pl.program_id(2) == 0))
    def _():
        for r in range(MM_TN // W_CAST_ROWS):
            lo = shift + r * W_CAST_ROWS
            hi = lo + W_CAST_ROWS
            if hi <= MM_TN:
                w = w_ref[0, lo:hi, :]
            else:
                w = jnp.concatenate([w_ref[0, lo:MM_TN, :], w2_ref[0, :hi - MM_TN, :]], axis=0)
            w16_ref[r * W_CAST_ROWS:(r + 1) * W_CAST_ROWS, :] = w.astype(BF16)

    nt = (((1,), (1,)), ((), ()))
    if segmented:
        a = a_ref[0].reshape(NSEG * PROJ_SEG_TILE, D_MODEL)
        r = lax.dot_general(a, w16_ref[...], nt, preferred_element_type=F32)
        o_ref[0] = jnp.swapaxes(r.reshape(NSEG, PROJ_SEG_TILE, MM_TN), 0, 1).astype(o_ref.dtype)
    else:
        o_ref[0] = lax.dot_general(a_ref[0], w16_ref[...], nt,
                                   preferred_element_type=F32).astype(o_ref.dtype)


def _proj(hn, w_t, layer, col0, w_col, n_col, out_dtype, segmented=False):
    shift = col0 % MM_TN
    base = col0 // MM_TN
    assert shift % 8 == 0 and MM_TN % max(shift, 1) == 0
    w_specs = [pl.BlockSpec((1, MM_TN, D_MODEL), lambda j, b, i: (layer, base + w_col(j), 0))]
    operands = [w_t]
    if shift:
        per = MM_TN // shift
        w_specs.append(pl.BlockSpec((1, shift, D_MODEL),
                                    lambda j, b, i: (layer, (base + w_col(j) + 1) * per, 0)))
        operands.append(w_t)
    if segmented:
        a = hn.reshape(BATCH, NSEG, SEG, D_MODEL)
        grid = (n_col, BATCH, SEG // PROJ_SEG_TILE)
        a_spec = pl.BlockSpec((1, NSEG, PROJ_SEG_TILE, D_MODEL), lambda j, b, i: (b, 0, i, 0))
        out_spec = pl.BlockSpec((1, PROJ_SEG_TILE, NSEG, MM_TN), lambda j, b, i: (b, i, 0, j))
        out_shape = jax.ShapeDtypeStruct((BATCH, SEG, NSEG, n_col * MM_TN), out_dtype)
    else:
        a = hn
        grid = (n_col, BATCH, SEQ // PROJ_TM)
        a_spec = pl.BlockSpec((1, PROJ_TM, D_MODEL), lambda j, b, i: (b, i, 0))
        out_spec = pl.BlockSpec((1, PROJ_TM, MM_TN), lambda j, b, i: (b, i, j))
        out_shape = jax.ShapeDtypeStruct((BATCH, SEQ, n_col * MM_TN), out_dtype)
    return pl.pallas_call(
        functools.partial(_proj_kernel, shift=shift, segmented=segmented),
        grid=grid,
        in_specs=[a_spec] + w_specs,
        out_specs=out_spec,
        out_shape=out_shape,
        scratch_shapes=[pltpu.VMEM((MM_TN, D_MODEL), BF16)],
        compiler_params=_cparams(3),
    )(a, *operands)


def _mlstm_kernel(qf_ref, ktf_ref, vf_ref, rowf_ref, colf_ref,
                  qb_ref, ktb_ref, vb_ref, rowb_ref, colb_ref,
                  hf_ref, hb_ref, ct_ref, m_ref):
    @pl.when(pl.program_id(2) == 0)
    def _():
        ct_ref[...] = jnp.zeros_like(ct_ref)
        m_ref[...] = jnp.zeros_like(m_ref)

    L = CHUNK
    jj = lax.broadcasted_iota(jnp.int32, (L, L), 0)
    ss = lax.broadcasted_iota(jnp.int32, (L, L), 1)
    lane = lax.broadcasted_iota(jnp.int32, (L, 128), 1)
    ones_blk = jnp.where(lane == 0, 1.0, 0.0).astype(BF16)

    dirs = ((qf_ref, ktf_ref, vf_ref, rowf_ref, colf_ref, hf_ref),
            (qb_ref, ktb_ref, vb_ref, rowb_ref, colb_ref, hb_ref))
    chains = [(hh, d) for hh in range(MLSTM_HPS) for d in range(2)]
    for hh, d in chains:
        q_ref, kt_ref, v_ref, row_ref, col_ref, out_ref = dirs[d]
        head = pl.program_id(1) * MLSTM_HPS + hh
        st = d * MLSTM_HPS + hh
        q = q_ref[0, :, hh * QK_DIM:(hh + 1) * QK_DIM]
        kt = kt_ref[hh * QK_DIM:(hh + 1) * QK_DIM, :]
        v_aug = jnp.concatenate([v_ref[0, :, hh * V_DIM:(hh + 1) * V_DIM], ones_blk],
                                axis=1)
        a_row = row_ref[pl.ds(HEADS * d + head, 1), :]
        b_row = row_ref[pl.ds(2 * HEADS + HEADS * d + head, 1), :]
        b_col = jnp.sum(jnp.where(lane == 2 * HEADS + HEADS * d + head, col_ref[...], 0.0),
                        axis=1, keepdims=True)
        m_prev = m_ref[st, 0:1, 0:1]

        causal = (ss <= jj) if d == 0 else (ss >= jj)
        d_log = jnp.where(causal, b_col + a_row, -jnp.inf)
        inter = b_col + m_prev
        m_row = jnp.maximum(inter, jnp.max(d_log, axis=1, keepdims=True))
        w_intra = jnp.exp(d_log - m_row)
        w_inter = jnp.exp(inter - m_row)

        s = jnp.dot(q, kt, preferred_element_type=F32)
        p = (s * w_intra).astype(BF16)
        ct = ct_ref[st]
        nd = (jnp.dot(p, v_aug, preferred_element_type=F32)
              + w_inter * jnp.dot(q, ct.astype(BF16), preferred_element_type=F32))
        num = nd[:, :V_DIM]
        den = nd[:, V_DIM:V_DIM + 1]
        inv = 1.0 / jnp.maximum(jnp.abs(den), jnp.exp(-m_row))
        out_ref[0, :, hh * V_DIM:(hh + 1) * V_DIM] = (num * inv).astype(out_ref.dtype)

        g_tot = b_row[:, L - 1:L] if d == 0 else b_row[:, 0:1]
        w_log = g_tot + a_row
        m_new = jnp.maximum(g_tot + m_prev, jnp.max(w_log, axis=1, keepdims=True))
        w_k = jnp.exp(w_log - m_new)
        decay = jnp.exp(g_tot + m_prev - m_new)
        ktw = (kt.astype(F32) * w_k).astype(BF16)
        ct_ref[st] = decay * ct + jnp.dot(ktw, v_aug, preferred_element_type=F32)
        m_ref[st] = jnp.broadcast_to(m_new, (8, 128))


def _mlstm(head, kt, grow, gcol):
    L, nc, hps = CHUNK, N_CHUNK, MLSTM_HPS
    qw, vw = hps * QK_DIM, hps * V_DIM
    rev = lambda c: nc - 1 - c

    def specs(cmap):
        return [
            pl.BlockSpec((1, L, qw), lambda b, h, c: (b, cmap(c), HEAD_Q // qw + h)),
            pl.BlockSpec((qw, L), lambda b, h, c: (h, b * nc + cmap(c))),
            pl.BlockSpec((1, L, vw), lambda b, h, c: (b, cmap(c), HEAD_V // vw + h)),
            pl.BlockSpec((N_GATE, L), lambda b, h, c: (0, b * nc + cmap(c))),
            pl.BlockSpec((L, 128), lambda b, h, c: (b * nc + cmap(c), 0)),
        ]

    fwd = lambda c: c
    out_sds = jax.ShapeDtypeStruct((BATCH, SEQ, M_WIDTH), BF16)
    return pl.pallas_call(
        _mlstm_kernel,
        grid=(BATCH, HEADS // hps, nc),
        in_specs=specs(fwd) + specs(rev),
        out_specs=[pl.BlockSpec((1, L, vw), lambda b, h, c: (b, c, h)),
                   pl.BlockSpec((1, L, vw), lambda b, h, c: (b, rev(c), h))],
        out_shape=[out_sds, out_sds],
        scratch_shapes=[pltpu.VMEM((2 * hps, QK_DIM, V_AUG), F32),
                        pltpu.VMEM((2 * hps, 8, 128), F32)],
        compiler_params=_cparams(3),
    )(head, kt, head, grow, gcol, head, kt, head, grow, gcol)


def _lru_kernel(x_ref, z_ref, cw_ref, cb_ref, w_ref, b_ref, lam_ref, o_ref,
                xs_ref, hf_ref, pf_ref, hb_ref, pb_ref, wh_ref, a_ref, u_ref):
    wc = x_ref.shape[-1]
    nb = wc // LRU_BLOCK
    tc = LRU_TC
    n_it = SEG // tc
    sub = lax.broadcasted_iota(jnp.int32, (NSEG, wc), 0)

    def copy_body(it, carry):
        t0 = pl.multiple_of(it * tc, tc)
        xs_ref[pl.ds(t0 + 2, tc)] = x_ref[0, pl.ds(t0, tc)]
        return carry

    lax.fori_loop(0, n_it, copy_body, 0)
    for r in range(2):
        prev = pltpu.roll(x_ref[0, SEG - 2 + r], 1, 0)
        xs_ref[r] = jnp.where(sub >= 1, prev, 0.0)
    nxt = pltpu.roll(x_ref[0, 0], NSEG - 1, 0)
    xs_ref[SEG + 2] = jnp.where(sub <= NSEG - 2, nxt, 0.0)

    cw = [0.5 * cw_ref[t:t + 1, :][None] for t in range(CONV_W)]
    cb = 0.5 * cb_ref[...][None]

    def conv_body(it, carry):
        t0 = pl.multiple_of(it * tc, tc)
        acc = xs_ref[pl.ds(t0, tc)] * cw[0]
        for t in range(1, CONV_W):
            acc = acc + xs_ref[pl.ds(t0 + t, tc)] * cw[t]
        xs_ref[pl.ds(t0, tc)] = cb + acc
        return carry

    lax.fori_loop(0, n_it, conv_body, 0)

    wh_ref[...] = jnp.zeros_like(wh_ref)
    wh_ref[:, :, :LRU_BLOCK, :] = w_ref[...]
    bh = 0.5 * b_ref[...]
    b_hi = bh.astype(BF16)
    b_lo = (bh - b_hi.astype(F32)).astype(BF16)
    wh_ref[:, :, LRU_BLOCK:LRU_BLOCK + 1, :] = b_hi
    wh_ref[:, :, LRU_BLOCK + 1:LRU_BLOCK + 2, :] = b_lo
    lane_blk = lax.broadcasted_iota(jnp.int32, (tc * NSEG, LRU_BLOCK), 1)
    ones_blk = jnp.where(lane_blk < 2, 1.0, 0.0).astype(BF16)
    hcs = (-0.5 * LRU_C) * _softplus(-lam_ref[...])

    def conv(t0):
        return xs_ref[pl.ds(t0, tc)]

    def gates(xc_half, d):
        hx = xc_half.reshape(tc * NSEG, wc)
        x16 = hx.astype(BF16)
        a_parts, u_parts = [], []
        for j in range(nb):
            cols = slice(j * LRU_BLOCK, (j + 1) * LRU_BLOCK)
            lhs = jnp.concatenate([x16[:, cols], ones_blk], axis=1)
            pre = jnp.dot(lhs, wh_ref[d, j], preferred_element_type=F32)
            t_r = jnp.tanh(pre[:, :LRU_BLOCK])
            t_i = jnp.tanh(pre[:, LRU_BLOCK:])
            h = hcs[d:d + 1, cols]
            log_a = h * t_r + h
            a_parts.append(jnp.exp(log_a))
            th = jnp.tanh(log_a)
            p = -2.0 * th
            q = 1.0 - th
            coef = jnp.where(p > 0.0, p * lax.rsqrt(p * q), 0.0)
            u_parts.append(coef * ((t_i + 1.0) * hx[:, cols]))
        a = jnp.concatenate(a_parts, axis=1).reshape(tc, NSEG, wc)
        u = jnp.concatenate(u_parts, axis=1).reshape(tc, NSEG, wc)
        return a, u

    def chunk_of(it, d):
        return it if d == 0 else n_it - 1 - it

    def gates_to(slot, it, d):
        t0 = pl.multiple_of(chunk_of(it, d) * tc, tc)
        a, u = gates(conv(t0), d)
        a_ref[d, slot] = a
        u_ref[d, slot] = u

    def half_step(it, carry, slot):
        cur = [(a_ref[d, slot], u_ref[d, slot]) for d in range(2)]
        nxt = jnp.minimum(it + 1, n_it - 1)
        for d in range(2):
            gates_to(1 - slot, nxt, d)
        out = []
        for d, (h_ref, p_ref) in enumerate(((hf_ref, pf_ref), (hb_ref, pb_ref))):
            h, p = carry[2 * d], carry[2 * d + 1]
            a, u = cur[d]
            t0 = pl.multiple_of(chunk_of(it, d) * tc, tc)
            hs, ps = [None] * tc, [None] * tc
            order = range(tc) if d == 0 else range(tc - 1, -1, -1)
            for k in order:
                h = a[k] * h + u[k]
                p = a[k] * p
                hs[k], ps[k] = h, p
            h_ref[pl.ds(t0, tc)] = jnp.stack(hs)
            p_ref[pl.ds(t0, tc)] = jnp.stack(ps)
            out += [h, p]
        return tuple(out)

    def scan_body(i2, carry):
        carry = half_step(2 * i2, carry, 0)
        return half_step(2 * i2 + 1, carry, 1)

    for d in range(2):
        gates_to(0, 0, d)
    zeros = jnp.zeros((NSEG, wc), F32)
    ones = jnp.ones((NSEG, wc), F32)
    hf_end, pf_end, hb_end, pb_end = lax.fori_loop(0, n_it // 2, scan_body,
                                                   (zeros, ones, zeros, ones))

    def carry_in(h_end, p_end, d):
        cin = jnp.zeros((NSEG, wc), F32)
        c = jnp.zeros((1, wc), F32)
        order = range(NSEG) if d == 0 else range(NSEG - 1, -1, -1)
        for s in order:
            cin = jnp.where(sub == s, c, cin)
            c = h_end[s:s + 1] + p_end[s:s + 1] * c
        return cin

    cin_f = carry_in(hf_end, pf_end, 0)
    cin_b = carry_in(hb_end, pb_end, 1)

    def out_body(it, carry):
        rows = pl.ds(pl.multiple_of(it * tc, tc), tc)
        z = z_ref[0, rows]
        h = (hf_ref[rows] + pf_ref[rows] * cin_f) + (hb_ref[rows] + pb_ref[rows] * cin_b)
        o_ref[0, rows] = (h * (z * _sigmoid(z))).astype(o_ref.dtype)
        return carry

    lax.fori_loop(0, n_it, out_body, 0)


def _lru(xz, cw, cb, w, b, lam):
    wc = LRU_WC
    nb = wc // LRU_BLOCK
    nj = LRU_W // wc
    blk = (1, SEG, NSEG, wc)
    seg_buf = pltpu.VMEM((SEG, NSEG, wc), F32)
    return pl.pallas_call(
        _lru_kernel,
        grid=(BATCH, nj),
        in_specs=[pl.BlockSpec(blk, lambda b_, j: (b_, 0, 0, j)),
                  pl.BlockSpec(blk, lambda b_, j: (b_, 0, 0, nj + j)),
                  pl.BlockSpec((CONV_W, wc), lambda b_, j: (0, j)),
                  pl.BlockSpec((1, wc), lambda b_, j: (0, j)),
                  pl.BlockSpec((2, nb, LRU_BLOCK, 2 * LRU_BLOCK), lambda b_, j: (0, j, 0, 0)),
                  pl.BlockSpec((2, nb, 1, 2 * LRU_BLOCK), lambda b_, j: (0, j, 0, 0)),
                  pl.BlockSpec((2, wc), lambda b_, j: (0, j))],
        out_specs=pl.BlockSpec(blk, lambda b_, j: (b_, 0, 0, j)),
        out_shape=jax.ShapeDtypeStruct((BATCH, SEG, NSEG, LRU_W), F32),
        scratch_shapes=[pltpu.VMEM((SEG + 3, NSEG, wc), F32),
                        seg_buf, seg_buf, seg_buf, seg_buf,
                        pltpu.VMEM((2, nb, 2 * LRU_BLOCK, 2 * LRU_BLOCK), BF16),
                        pltpu.VMEM((2, 2, LRU_TC, NSEG, wc), F32),
                        pltpu.VMEM((2, 2, LRU_TC, NSEG, wc), F32)],
        compiler_params=_cparams(2),
    )(xz, xz, cw, cb, w, b, lam)


def _branch_a_kernel(hf_ref, hb_ref, o_ref, za_ref, hg_ref, ga_ref, w_ref, out_ref, ya_ref):
    tm = ya_ref.shape[0]
    acc = None
    for hh in range(HEADS):
        cols = slice(hh * V_DIM, (hh + 1) * V_DIM)
        hg = hg_ref[:, cols]
        for r in range(tm // NORM_ROWS):
            rows = slice(r * NORM_ROWS, (r + 1) * NORM_ROWS)
            hs = hf_ref[0, rows, cols].astype(F32) + hb_ref[0, rows, cols].astype(F32)
            ms = jnp.mean(hs * hs, axis=-1, keepdims=True)
            hn = hs * lax.rsqrt(ms + NORM_EPS) * hg
            o = o_ref[0, rows, cols]
            z = za_ref[0, rows, cols]
            gate = (0.25 * z) * ((1.0 + jnp.tanh(0.5 * o)) * (1.0 + jnp.tanh(0.5 * z)))
            ya_ref[rows, cols] = (hn * gate.astype(F32)).astype(BF16)
        part = jnp.dot(ya_ref[:, cols], w_ref[0, cols, :], preferred_element_type=F32)
        acc = part if acc is None else acc + part
    out_ref[0] = (_sigmoid(ga_ref[0]).astype(F32) * acc).astype(out_ref.dtype)


def _branch_a(hf, hb, head, gg, hg, w, layer, tm):
    wide = (1, tm, M_WIDTH)
    return pl.pallas_call(
        _branch_a_kernel,
        grid=(BATCH, SEQ // tm),
        in_specs=[pl.BlockSpec(wide, lambda b, i: (b, i, 0)),
                  pl.BlockSpec(wide, lambda b, i: (b, i, 0)),
                  pl.BlockSpec(wide, lambda b, i: (b, i, HEAD_O // M_WIDTH)),
                  pl.BlockSpec(wide, lambda b, i: (b, i, HEAD_ZA // M_WIDTH)),
                  pl.BlockSpec((1, M_WIDTH), lambda b, i: (0, 0)),
                  pl.BlockSpec((1, tm, D_MODEL), lambda b, i: (b, i, 0)),
                  pl.BlockSpec((1, M_WIDTH, D_MODEL), lambda b, i: (layer, 0, 0))],
        out_specs=pl.BlockSpec((1, tm, D_MODEL), lambda b, i: (b, i, 0)),
        out_shape=jax.ShapeDtypeStruct((BATCH, SEQ, D_MODEL), BF16),
        scratch_shapes=[pltpu.VMEM((tm, M_WIDTH), BF16)],
        compiler_params=_cparams(2),
    )(hf, hb, head, head, hg, gg, w)


def _branch_b_kernel(yb_ref, gb_ref, pa_ref, w_ref, out_ref, y16_ref):
    rows = NSEG * SEG_TILE
    step = 512
    for c in range(LRU_W // step):
        cols = slice(c * step, (c + 1) * step)
        y = jnp.swapaxes(yb_ref[0, :, :, cols], 0, 1)
        y16_ref[:, cols] = y.reshape(rows, step).astype(BF16)
    acc = jnp.dot(y16_ref[...], w_ref[0], preferred_element_type=F32)
    pa = pa_ref[0].reshape(rows, D_MODEL).astype(F32)
    gb = _sigmoid(gb_ref[0].reshape(rows, D_MODEL)).astype(F32)
    out_ref[0] = (pa + gb * acc).astype(out_ref.dtype).reshape(NSEG, SEG_TILE, D_MODEL)


def _branch_b(yb, gg, part_a, w, layer):
    seg4 = lambda arr: arr.reshape(BATCH, NSEG, SEG, arr.shape[-1])
    tile = (1, NSEG, SEG_TILE, D_MODEL)
    merged = pl.pallas_call(
        _branch_b_kernel,
        grid=(BATCH, SEG // SEG_TILE),
        in_specs=[pl.BlockSpec((1, SEG_TILE, NSEG, LRU_W), lambda b, i: (b, i, 0, 0)),
                  pl.BlockSpec(tile, lambda b, i: (b, 0, i, 1)),
                  pl.BlockSpec(tile, lambda b, i: (b, 0, i, 0)),
                  pl.BlockSpec((1, LRU_W, D_MODEL), lambda b, i: (layer, 0, 0))],
        out_specs=pl.BlockSpec(tile, lambda b, i: (b, 0, i, 0)),
        out_shape=jax.ShapeDtypeStruct((BATCH, NSEG, SEG, D_MODEL), BF16),
        scratch_shapes=[pltpu.VMEM((NSEG * SEG_TILE, LRU_W), BF16)],
        compiler_params=_cparams(2),
    )(yb, seg4(gg), seg4(part_a), w)
    return merged.reshape(BATCH, SEQ, D_MODEL)


def _out_kernel(m_ref, x_ref, w_ref, fg_ref, o_ref, *, final_norm):
    y = x_ref[0] + jnp.dot(m_ref[0], w_ref[0], preferred_element_type=F32)
    if final_norm:
        ms = jnp.mean(y * y, axis=-1, keepdims=True)
        y = y * lax.rsqrt(ms + NORM_EPS) * fg_ref[...]
    o_ref[0] = y


def _out_proj(merged, x, w, layer, fg, final_norm, tm):
    row = lambda b, i: (b, i, 0)
    return pl.pallas_call(
        functools.partial(_out_kernel, final_norm=final_norm),
        grid=(BATCH, SEQ // tm),
        in_specs=[pl.BlockSpec((1, tm, D_MODEL), row),
                  pl.BlockSpec((1, tm, D_MODEL), row),
                  pl.BlockSpec((1, D_MODEL, D_MODEL), lambda b, i: (layer, 0, 0)),
                  pl.BlockSpec((1, D_MODEL), lambda b, i: (0, 0))],
        out_specs=pl.BlockSpec((1, tm, D_MODEL), row),
        out_shape=jax.ShapeDtypeStruct((BATCH, SEQ, D_MODEL), F32),
        compiler_params=_cparams(2),
    )(merged, x, w, fg)


def kernel(x, norm_g, w_in, b_if, head_g, conv_w, conv_b, w_rg, b_rg, lru_lambda,
           w_branch_a, w_branch_b, w_out, final_g):
    w_t = jnp.swapaxes(w_in, 1, 2)
    bg = jnp.pad(b_if, ((0, 0), (0, 128 - N_GATE)))
    n_blk = LRU_W // LRU_BLOCK
    w_gate = jnp.transpose(w_rg, (0, 1, 3, 4, 2, 5)).reshape(
        DEPTH, 2, n_blk, LRU_BLOCK, 2 * LRU_BLOCK).astype(BF16)
    b_gate = jnp.transpose(b_rg.reshape(DEPTH, 2, 2, n_blk, LRU_BLOCK), (0, 1, 3, 2, 4)).reshape(
        DEPTH, 2, n_blk, 1, 2 * LRU_BLOCK)
    w_a16 = w_branch_a.astype(BF16)
    w_b16 = w_branch_b.astype(BF16)
    w_o16 = w_out.astype(BF16)

    head_col = lambda j: jnp.where(j < 4, j + 4, jnp.where(j < 6, j - 2, 0))
    xz_cols = 2 * LRU_W // MM_TN

    h = x
    for l in range(DEPTH):
        hn, kt, grow, gcol = _kgate(h, norm_g[l][None, :], w_t, l, b_if[l][:, None],
                                    bg[l][None, :], tm=1024)
        head = _proj(hn, w_t, l, 0, head_col, N_HEAD // MM_TN, BF16)
        gg = _proj(hn, w_t, l, _TAIL0, lambda j: xz_cols + j, 2 * D_MODEL // MM_TN, BF16)
        xz = _proj(hn, w_t, l, _TAIL0, lambda j: j, xz_cols, F32, segmented=True)

        hf, hb = _mlstm(head, kt, grow, gcol)
        yb = _lru(xz, conv_w[l], conv_b[l][None, :], w_gate[l], b_gate[l], lru_lambda[l])

        part_a = _branch_a(hf, hb, head, gg, head_g[l][None, :], w_a16, l, tm=512)
        merged = _branch_b(yb, gg, part_a, w_b16, l)
        h = _out_proj(merged, h, w_o16, l, final_g[None, :], l == DEPTH - 1, tm=512)
    return h
```

```python
import functools

import jax
import jax.numpy as jnp
from jax import lax
from jax.experimental import pallas as pl
from jax.experimental.pallas import tpu as pltpu

F32 = jnp.float32
BF16 = jnp.bfloat16

D_MODEL = 2048
BATCH = 4
SEQ = 4096
DEPTH = 2
HEADS = 4
QK_DIM = 256
V_DIM = 512
QK_WIDTH = HEADS * QK_DIM
M_WIDTH = HEADS * V_DIM
N_GATE = 4 * HEADS
LRU_W = D_MODEL
LRU_BLOCK = 128
LRU_C = 8.0
CONV_W = 4
NORM_EPS = 1e-6
QK_SCALE = QK_DIM ** -0.5

_K0, _V0 = 1024, 2048
_G0 = 8192
_TAIL0 = _G0 + N_GATE

HEAD_O, HEAD_ZA, HEAD_V, HEAD_Q = 0, 2048, 4096, 6144
N_HEAD = 7168

CHUNK = 256
N_CHUNK = SEQ // CHUNK
V_AUG = V_DIM + 128
MLSTM_HPS = 4

NSEG = 8
SEG = SEQ // NSEG
LRU_WC = 256
LRU_TC = 32

SEG_TILE = 64
PROJ_SEG_TILE = 128

NORM_ROWS = 128
MM_TN = 1024
PROJ_TM = 2048
VMEM_LIMIT = 56 * 1024 * 1024


def _cparams(n_axes):
    return pltpu.CompilerParams(dimension_semantics=("arbitrary",) * n_axes,
                                vmem_limit_bytes=VMEM_LIMIT)


def _sigmoid(x):
    return 0.5 * jnp.tanh(0.5 * x) + 0.5


def _log_sigmoid(x):
    return jnp.minimum(x, 0.0) - jnp.log1p(jnp.exp(-jnp.abs(x)))


def _softplus(x):
    return jnp.maximum(x, 0.0) + jnp.log1p(jnp.exp(-jnp.abs(x)))


def _chunk_cumsum(x, axis, reverse):
    n = x.shape[axis]
    idx = lax.broadcasted_iota(jnp.int32, x.shape, axis) % CHUNK
    d = 1
    while d < CHUNK:
        if reverse:
            x = x + jnp.where(idx < CHUNK - d, pltpu.roll(x, n - d, axis), 0.0)
        else:
            x = x + jnp.where(idx >= d, pltpu.roll(x, d, axis), 0.0)
        d *= 2
    return x


def _kgate_kernel(x_ref, g_ref, wk_ref, wg_ref, bcol_ref, brow_ref,
                  hn_ref, kt_ref, grow_ref, gcol_ref, wkt_ref, wg16_ref):
    @pl.when((pl.program_id(0) == 0) & (pl.program_id(1) == 0))
    def _():
        step = 256
        for c in range(QK_WIDTH // step):
            wkt_ref[c * step:(c + 1) * step, :] = wk_ref[0, c * step:(c + 1) * step, :].astype(BF16)
        wg16_ref[...] = jnp.zeros_like(wg16_ref)
        wg16_ref[:N_GATE, :] = wg_ref[0].astype(BF16)
        wkt_ref[QK_WIDTH:, :] = wg16_ref[...]

    g = g_ref[...]

    def norm_body(r, carry):
        rows = pl.ds(pl.multiple_of(r * NORM_ROWS, NORM_ROWS), NORM_ROWS)
        x = x_ref[0, rows, :]
        ms = jnp.mean(x * x, axis=-1, keepdims=True)
        hn_ref[0, rows, :] = (x * lax.rsqrt(ms + NORM_EPS) * g).astype(BF16)
        return carry

    lax.fori_loop(0, hn_ref.shape[1] // NORM_ROWS, norm_body, 0)
    hn = hn_ref[0]
    nt = (((1,), (1,)), ((), ()))
    kg = lax.dot_general(wkt_ref[...], hn, nt, preferred_element_type=F32)
    kt_ref[...] = (kg[:QK_WIDTH] * QK_SCALE).astype(BF16)

    wg = wg16_ref[...]
    xc = lax.dot_general(hn, wg, nt, preferred_element_type=F32) + brow_ref[...]
    lfc = _log_sigmoid(xc)
    lane = lax.broadcasted_iota(jnp.int32, lfc.shape, 1)
    gcol_ref[...] = jnp.where(lane < 3 * HEADS, _chunk_cumsum(lfc, 0, False),
                              _chunk_cumsum(lfc, 0, True))

    xr = kg[QK_WIDTH:QK_WIDTH + N_GATE] + bcol_ref[...]
    lf = _log_sigmoid(xr[2 * HEADS:])
    r8 = lax.broadcasted_iota(jnp.int32, lf.shape, 0)
    cum = jnp.where(r8 < HEADS, _chunk_cumsum(lf, 1, False), _chunk_cumsum(lf, 1, True))
    grow_ref[:2 * HEADS, :] = xr[:2 * HEADS] - cum
    grow_ref[2 * HEADS:, :] = cum


def _kgate(x, g, w_t, layer, bcol, brow, tm):
    nt = SEQ // tm
    tok = BATCH * SEQ
    full = lambda shape: pl.BlockSpec(shape, lambda b, i: (0,) * len(shape))
    return pl.pallas_call(
        _kgate_kernel,
        grid=(BATCH, nt),
        in_specs=[pl.BlockSpec((1, tm, D_MODEL), lambda b, i: (b, i, 0)),
                  full((1, D_MODEL)),
                  pl.BlockSpec((1, QK_WIDTH, D_MODEL), lambda b, i: (layer, _K0 // QK_WIDTH, 0)),
                  pl.BlockSpec((1, N_GATE, D_MODEL), lambda b, i: (layer, _G0 // N_GATE, 0)),
                  full((N_GATE, 1)), full((1, 128))],
        out_specs=[pl.BlockSpec((1, tm, D_MODEL), lambda b, i: (b, i, 0)),
                   pl.BlockSpec((QK_WIDTH, tm), lambda b, i: (0, b * nt + i)),
                   pl.BlockSpec((N_GATE, tm), lambda b, i: (0, b * nt + i)),
                   pl.BlockSpec((tm, 128), lambda b, i: (b * nt + i, 0))],
        out_shape=[jax.ShapeDtypeStruct((BATCH, SEQ, D_MODEL), BF16),
                   jax.ShapeDtypeStruct((QK_WIDTH, tok), BF16),
                   jax.ShapeDtypeStruct((N_GATE, tok), F32),
                   jax.ShapeDtypeStruct((tok, 128), F32)],
        scratch_shapes=[pltpu.VMEM((QK_WIDTH + 128, D_MODEL), BF16),
                        pltpu.VMEM((128, D_MODEL), BF16)],
        compiler_params=_cparams(2),
    )(x, g, w_t, w_t, bcol, brow)


W_CAST_ROWS = 128


def _proj_kernel(*refs, shift, segmented):
    if shift:
        a_ref, w_ref, w2_ref, o_ref, w16_ref = refs
    else:
        a_ref, w_ref, o_ref, w16_ref = refs

    @pl.when((pl.program_id(1) == 0) & (pl.program_id(2) == 0))
    def _():
        for r in range(MM_TN // W_CAST_ROWS):
            lo = shift + r * W_CAST_ROWS
            hi = lo + W_CAST_ROWS
            if hi <= MM_TN:
                w = w_ref[0, lo:hi, :]
            else:
                w = jnp.concatenate([w_ref[0, lo:MM_TN, :], w2_ref[0, :hi - MM_TN, :]], axis=0)
            w16_ref[r * W_CAST_ROWS:(r + 1) * W_CAST_ROWS, :] = w.astype(BF16)

    nt = (((1,), (1,)), ((), ()))
    if segmented:
        a = a_ref[0].reshape(NSEG * PROJ_SEG_TILE, D_MODEL)
        r = lax.dot_general(a, w16_ref[...], nt, preferred_element_type=F32)
        o_ref[0] = jnp.swapaxes(r.reshape(NSEG, PROJ_SEG_TILE, MM_TN), 0, 1).astype(o_ref.dtype)
    else:
        o_ref[0] = lax.dot_general(a_ref[0], w16_ref[...], nt,
                                   preferred_element_type=F32).astype(o_ref.dtype)


def _proj(hn, w_t, layer, col0, w_col, n_col, out_dtype, segmented=False):
    shift = col0 % MM_TN
    base = col0 // MM_TN
    assert shift % 8 == 0 and MM_TN % max(shift, 1) == 0
    w_specs = [pl.BlockSpec((1, MM_TN, D_MODEL), lambda j, b, i: (layer, base + w_col(j), 0))]
    operands = [w_t]
    if shift:
        per = MM_TN // shift
        w_specs.append(pl.BlockSpec((1, shift, D_MODEL),
                                    lambda j, b, i: (layer, (base + w_col(j) + 1) * per, 0)))
        operands.append(w_t)
    if segmented:
        a = hn.reshape(BATCH, NSEG, SEG, D_MODEL)
        grid = (n_col, BATCH, SEG // PROJ_SEG_TILE)
        a_spec = pl.BlockSpec((1, NSEG, PROJ_SEG_TILE, D_MODEL), lambda j, b, i: (b, 0, i, 0))
        out_spec = pl.BlockSpec((1, PROJ_SEG_TILE, NSEG, MM_TN), lambda j, b, i: (b, i, 0, j))
        out_shape = jax.ShapeDtypeStruct((BATCH, SEG, NSEG, n_col * MM_TN), out_dtype)
    else:
        a = hn
        grid = (n_col, BATCH, SEQ // PROJ_TM)
        a_spec = pl.BlockSpec((1, PROJ_TM, D_MODEL), lambda j, b, i: (b, i, 0))
        out_spec = pl.BlockSpec((1, PROJ_TM, MM_TN), lambda j, b, i: (b, i, j))
        out_shape = jax.ShapeDtypeStruct((BATCH, SEQ, n_col * MM_TN), out_dtype)
    return pl.pallas_call(
        functools.partial(_proj_kernel, shift=shift, segmented=segmented),
        grid=grid,
        in_specs=[a_spec] + w_specs,
        out_specs=out_spec,
        out_shape=out_shape,
        scratch_shapes=[pltpu.VMEM((MM_TN, D_MODEL), BF16)],
        compiler_params=_cparams(3),
    )(a, *operands)


def _mlstm_kernel(qf_ref, ktf_ref, vf_ref, rowf_ref, colf_ref,
                  qb_ref, ktb_ref, vb_ref, rowb_ref, colb_ref,
                  hf_ref, hb_ref, ct_ref, m_ref):
    @pl.when(pl.program_id(2) == 0)
    def _():
        ct_ref[...] = jnp.zeros_like(ct_ref)
        m_ref[...] = jnp.zeros_like(m_ref)

    L = CHUNK
    jj = lax.broadcasted_iota(jnp.int32, (L, L), 0)
    ss = lax.broadcasted_iota(jnp.int32, (L, L), 1)
    lane = lax.broadcasted_iota(jnp.int32, (L, 128), 1)
    ones_blk = jnp.where(lane == 0, 1.0, 0.0).astype(BF16)

    dirs = ((qf_ref, ktf_ref, vf_ref, rowf_ref, colf_ref, hf_ref),
            (qb_ref, ktb_ref, vb_ref, rowb_ref, colb_ref, hb_ref))
    chains = [(hh, d) for hh in range(MLSTM_HPS) for d in range(2)]
    for hh, d in chains:
        q_ref, kt_ref, v_ref, row_ref, col_ref, out_ref = dirs[d]
        head = pl.program_id(1) * MLSTM_HPS + hh
        st = d * MLSTM_HPS + hh
        q = q_ref[0, :, hh * QK_DIM:(hh + 1) * QK_DIM]
        kt = kt_ref[hh * QK_DIM:(hh + 1) * QK_DIM, :]
        v_aug = jnp.concatenate([v_ref[0, :, hh * V_DIM:(hh + 1) * V_DIM], ones_blk],
                                axis=1)
        a_row = row_ref[pl.ds(HEADS * d + head, 1), :]
        b_row = row_ref[pl.ds(2 * HEADS + HEADS * d + head, 1), :]
        b_col = jnp.sum(jnp.where(lane == 2 * HEADS + HEADS * d + head, col_ref[...], 0.0),
                        axis=1, keepdims=True)
        m_prev = m_ref[st, 0:1, 0:1]

        causal = (ss <= jj) if d == 0 else (ss >= jj)
        d_log = jnp.where(causal, b_col + a_row, -jnp.inf)
        inter = b_col + m_prev
        m_row = jnp.maximum(inter, jnp.max(d_log, axis=1, keepdims=True))
        w_intra = jnp.exp(d_log - m_row)
        w_inter = jnp.exp(inter - m_row)

        s = jnp.dot(q, kt, preferred_element_type=F32)
        p = (s * w_intra).astype(BF16)
        ct = ct_ref[st]
        nd = (jnp.dot(p, v_aug, preferred_element_type=F32)
              + w_inter * jnp.dot(q, ct.astype(BF16), preferred_element_type=F32))
        num = nd[:, :V_DIM]
        den = nd[:, V_DIM:V_DIM + 1]
        inv = 1.0 / jnp.maximum(jnp.abs(den), jnp.exp(-m_row))
        out_ref[0, :, hh * V_DIM:(hh + 1) * V_DIM] = (num * inv).astype(out_ref.dtype)

        g_tot = b_row[:, L - 1:L] if d == 0 else b_row[:, 0:1]
        w_log = g_tot + a_row
        m_new = jnp.maximum(g_tot + m_prev, jnp.max(w_log, axis=1, keepdims=True))
        w_k = jnp.exp(w_log - m_new)
        decay = jnp.exp(g_tot + m_prev - m_new)
        ktw = (kt.astype(F32) * w_k).astype(BF16)
        ct_ref[st] = decay * ct + jnp.dot(ktw, v_aug, preferred_element_type=F32)
        m_ref[st] = jnp.broadcast_to(m_new, (8, 128))


def _mlstm(head, kt, grow, gcol):
    L, nc, hps = CHUNK, N_CHUNK, MLSTM_HPS
    qw, vw = hps * QK_DIM, hps * V_DIM
    rev = lambda c: nc - 1 - c

    def specs(cmap):
        return [
            pl.BlockSpec((1, L, qw), lambda b, h, c: (b, cmap(c), HEAD_Q // qw + h)),
            pl.BlockSpec((qw, L), lambda b, h, c: (h, b * nc + cmap(c))),
            pl.BlockSpec((1, L, vw), lambda b, h, c: (b, cmap(c), HEAD_V // vw + h)),
            pl.BlockSpec((N_GATE, L), lambda b, h, c: (0, b * nc + cmap(c))),
            pl.BlockSpec((L, 128), lambda b, h, c: (b * nc + cmap(c), 0)),
        ]

    fwd = lambda c: c
    out_sds = jax.ShapeDtypeStruct((BATCH, SEQ, M_WIDTH), BF16)
    return pl.pallas_call(
        _mlstm_kernel,
        grid=(BATCH, HEADS // hps, nc),
        in_specs=specs(fwd) + specs(rev),
        out_specs=[pl.BlockSpec((1, L, vw), lambda b, h, c: (b, c, h)),
                   pl.BlockSpec((1, L, vw), lambda b, h, c: (b, rev(c), h))],
        out_shape=[out_sds, out_sds],
        scratch_shapes=[pltpu.VMEM((2 * hps, QK_DIM, V_AUG), F32),
                        pltpu.VMEM((2 * hps, 8, 128), F32)],
        compiler_params=_cparams(3),
    )(head, kt, head, grow, gcol, head, kt, head, grow, gcol)


def _lru_kernel(x_ref, z_ref, cw_ref, cb_ref, w_ref, b_ref, lam_ref, o_ref,
                xs_ref, hf_ref, pf_ref, hb_ref, pb_ref, wh_ref, a_ref, u_ref):
    wc = x_ref.shape[-1]
    nb = wc // LRU_BLOCK
    tc = LRU_TC
    n_it = SEG // tc
    sub = lax.broadcasted_iota(jnp.int32, (NSEG, wc), 0)

    def copy_body(it, carry):
        t0 = pl.multiple_of(it * tc, tc)
        xs_ref[pl.ds(t0 + 2, tc)] = x_ref[0, pl.ds(t0, tc)]
        return carry

    lax.fori_loop(0, n_it, copy_body, 0)
    for r in range(2):
        prev = pltpu.roll(x_ref[0, SEG - 2 + r], 1, 0)
        xs_ref[r] = jnp.where(sub >= 1, prev, 0.0)
    nxt = pltpu.roll(x_ref[0, 0], NSEG - 1, 0)
    xs_ref[SEG + 2] = jnp.where(sub <= NSEG - 2, nxt, 0.0)

    cw = [0.5 * cw_ref[t:t + 1, :][None] for t in range(CONV_W)]
    cb = 0.5 * cb_ref[...][None]

    def conv_body(it, carry):
        t0 = pl.multiple_of(it * tc, tc)
        acc = xs_ref[pl.ds(t0, tc)] * cw[0]
        for t in range(1, CONV_W):
            acc = acc + xs_ref[pl.ds(t0 + t, tc)] * cw[t]
        xs_ref[pl.ds(t0, tc)] = cb + acc
        return carry

    lax.fori_loop(0, n_it, conv_body, 0)

    wh_ref[...] = jnp.zeros_like(wh_ref)
    wh_ref[:, :, :LRU_BLOCK, :] = w_ref[...]
    bh = 0.5 * b_ref[...]
    b_hi = bh.astype(BF16)
    b_lo = (bh - b_hi.astype(F32)).astype(BF16)
    wh_ref[:, :, LRU_BLOCK:LRU_BLOCK + 1, :] = b_hi
    wh_ref[:, :, LRU_BLOCK + 1:LRU_BLOCK + 2, :] = b_lo
    lane_blk = lax.broadcasted_iota(jnp.int32, (tc * NSEG, LRU_BLOCK), 1)
    ones_blk = jnp.where(lane_blk < 2, 1.0, 0.0).astype(BF16)
    hcs = (-0.5 * LRU_C) * _softplus(-lam_ref[...])

    def conv(t0):
        return xs_ref[pl.ds(t0, tc)]

    def gates(xc_half, d):
        hx = xc_half.reshape(tc * NSEG, wc)
        x16 = hx.astype(BF16)
        a_parts, u_parts = [], []
        for j in range(nb):
            cols = slice(j * LRU_BLOCK, (j + 1) * LRU_BLOCK)
            lhs = jnp.concatenate([x16[:, cols], ones_blk], axis=1)
            pre = jnp.dot(lhs, wh_ref[d, j], preferred_element_type=F32)
            t_r = jnp.tanh(pre[:, :LRU_BLOCK])
            t_i = jnp.tanh(pre[:, LRU_BLOCK:])
            h = hcs[d:d + 1, cols]
            log_a = h * t_r + h
            a_parts.append(jnp.exp(log_a))
            th = jnp.tanh(log_a)
            p = -2.0 * th
            q = 1.0 - th
            coef = jnp.where(p > 0.0, p * lax.rsqrt(p * q), 0.0)
            u_parts.append(coef * ((t_i + 1.0) * hx[:, cols]))
        a = jnp.concatenate(a_parts, axis=1).reshape(tc, NSEG, wc)
        u = jnp.concatenate(u_parts, axis=1).reshape(tc, NSEG, wc)
        return a, u

    def chunk_of(it, d):
        return it if d == 0 else n_it - 1 - it

    def gates_to(slot, it, d):
        t0 = pl.multiple_of(chunk_of(it, d) * tc, tc)
        a, u = gates(conv(t0), d)
        a_ref[d, slot] = a
        u_ref[d, slot] = u

    def half_step(it, carry, slot):
        cur = [(a_ref[d, slot], u_ref[d, slot]) for d in range(2)]
        nxt = jnp.minimum(it + 1, n_it - 1)
        for d in range(2):
            gates_to(1 - slot, nxt, d)
        out = []
        for d, (h_ref, p_ref) in enumerate(((hf_ref, pf_ref), (hb_ref, pb_ref))):
            h, p = carry[2 * d], carry[2 * d + 1]
            a, u = cur[d]
            t0 = pl.multiple_of(chunk_of(it, d) * tc, tc)
            hs, ps = [None] * tc, [None] * tc
            order = range(tc) if d == 0 else range(tc - 1, -1, -1)
            for k in order:
                h = a[k] * h + u[k]
                p = a[k] * p
                hs[k], ps[k] = h, p
            h_ref[pl.ds(t0, tc)] = jnp.stack(hs)
            p_ref[pl.ds(t0, tc)] = jnp.stack(ps)
            out += [h, p]
        return tuple(out)

    def scan_body(i2, carry):
        carry = half_step(2 * i2, carry, 0)
        return half_step(2 * i2 + 1, carry, 1)

    for d in range(2):
        gates_to(0, 0, d)
    zeros = jnp.zeros((NSEG, wc), F32)
    ones = jnp.ones((NSEG, wc), F32)
    hf_end, pf_end, hb_end, pb_end = lax.fori_loop(0, n_it // 2, scan_body,
                                                   (zeros, ones, zeros, ones))

    def carry_in(h_end, p_end, d):
        cin = jnp.zeros((NSEG, wc), F32)
        c = jnp.zeros((1, wc), F32)
        order = range(NSEG) if d == 0 else range(NSEG - 1, -1, -1)
        for s in order:
            cin = jnp.where(sub == s, c, cin)
            c = h_end[s:s + 1] + p_end[s:s + 1] * c
        return cin

    cin_f = carry_in(hf_end, pf_end, 0)
    cin_b = carry_in(hb_end, pb_end, 1)

    def out_body(it, carry):
        rows = pl.ds(pl.multiple_of(it * tc, tc), tc)
        z = z_ref[0, rows]
        h = (hf_ref[rows] + pf_ref[rows] * cin_f) + (hb_ref[rows] + pb_ref[rows] * cin_b)
        hz = 0.5 * z
        o_ref[0, rows] = (h * (hz * (1.0 + jnp.tanh(hz)))).astype(o_ref.dtype)
        return carry

    lax.fori_loop(0, n_it, out_body, 0)


def _lru(xz, cw, cb, w, b, lam):
    wc = LRU_WC
    nb = wc // LRU_BLOCK
    nj = LRU_W // wc
    blk = (1, SEG, NSEG, wc)
    seg_buf = pltpu.VMEM((SEG, NSEG, wc), F32)
    return pl.pallas_call(
        _lru_kernel,
        grid=(BATCH, nj),
        in_specs=[pl.BlockSpec(blk, lambda b_, j: (b_, 0, 0, j)),
                  pl.BlockSpec(blk, lambda b_, j: (b_, 0, 0, nj + j)),
                  pl.BlockSpec((CONV_W, wc), lambda b_, j: (0, j)),
                  pl.BlockSpec((1, wc), lambda b_, j: (0, j)),
                  pl.BlockSpec((2, nb, LRU_BLOCK, 2 * LRU_BLOCK), lambda b_, j: (0, j, 0, 0)),
                  pl.BlockSpec((2, nb, 1, 2 * LRU_BLOCK), lambda b_, j: (0, j, 0, 0)),
                  pl.BlockSpec((2, wc), lambda b_, j: (0, j))],
        out_specs=pl.BlockSpec(blk, lambda b_, j: (b_, 0, 0, j)),
        out_shape=jax.ShapeDtypeStruct((BATCH, SEG, NSEG, LRU_W), F32),
        scratch_shapes=[pltpu.VMEM((SEG + 3, NSEG, wc), F32),
                        seg_buf, seg_buf, seg_buf, seg_buf,
                        pltpu.VMEM((2, nb, 2 * LRU_BLOCK, 2 * LRU_BLOCK), BF16),
                        pltpu.VMEM((2, 2, LRU_TC, NSEG, wc), F32),
                        pltpu.VMEM((2, 2, LRU_TC, NSEG, wc), F32)],
        compiler_params=_cparams(2),
    )(xz, xz, cw, cb, w, b, lam)


def _branch_a_kernel(hf_ref, hb_ref, o_ref, za_ref, hg_ref, ga_ref, w_ref, out_ref, ya_ref):
    tm = ya_ref.shape[0]
    acc = None
    for hh in range(HEADS):
        cols = slice(hh * V_DIM, (hh + 1) * V_DIM)
        hg = hg_ref[:, cols]
        for r in range(tm // NORM_ROWS):
            rows = slice(r * NORM_ROWS, (r + 1) * NORM_ROWS)
            hs = hf_ref[0, rows, cols].astype(F32) + hb_ref[0, rows, cols].astype(F32)
            ms = jnp.mean(hs * hs, axis=-1, keepdims=True)
            hn = hs * lax.rsqrt(ms + NORM_EPS) * hg
            o = o_ref[0, rows, cols]
            z = za_ref[0, rows, cols]
            gate = (0.25 * z) * ((1.0 + jnp.tanh(0.5 * o)) * (1.0 + jnp.tanh(0.5 * z)))
            ya_ref[rows, cols] = (hn * gate.astype(F32)).astype(BF16)
        part = jnp.dot(ya_ref[:, cols], w_ref[0, cols, :], preferred_element_type=F32)
        acc = part if acc is None else acc + part
    out_ref[0] = (_sigmoid(ga_ref[0]).astype(F32) * acc).astype(out_ref.dtype)


def _branch_a(hf, hb, head, gg, hg, w, layer, tm):
    wide = (1, tm, M_WIDTH)
    return pl.pallas_call(
        _branch_a_kernel,
        grid=(BATCH, SEQ // tm),
        in_specs=[pl.BlockSpec(wide, lambda b, i: (b, i, 0)),
                  pl.BlockSpec(wide, lambda b, i: (b, i, 0)),
                  pl.BlockSpec(wide, lambda b, i: (b, i, HEAD_O // M_WIDTH)),
                  pl.BlockSpec(wide, lambda b, i: (b, i, HEAD_ZA // M_WIDTH)),
                  pl.BlockSpec((1, M_WIDTH), lambda b, i: (0, 0)),
                  pl.BlockSpec((1, tm, D_MODEL), lambda b, i: (b, i, 0)),
                  pl.BlockSpec((1, M_WIDTH, D_MODEL), lambda b, i: (layer, 0, 0))],
        out_specs=pl.BlockSpec((1, tm, D_MODEL), lambda b, i: (b, i, 0)),
        out_shape=jax.ShapeDtypeStruct((BATCH, SEQ, D_MODEL), BF16),
        scratch_shapes=[pltpu.VMEM((tm, M_WIDTH), BF16)],
        compiler_params=_cparams(2),
    )(hf, hb, head, head, hg, gg, w)


def _branch_b_kernel(yb_ref, gb_ref, pa_ref, w_ref, out_ref, y16_ref):
    rows = NSEG * SEG_TILE
    step = 512
    for c in range(LRU_W // step):
        cols = slice(c * step, (c + 1) * step)
        y = jnp.swapaxes(yb_ref[0, :, :, cols], 0, 1)
        y16_ref[:, cols] = y.reshape(rows, step).astype(BF16)
    acc = jnp.dot(y16_ref[...], w_ref[0], preferred_element_type=F32)
    pa = pa_ref[0].reshape(rows, D_MODEL).astype(F32)
    gb = _sigmoid(gb_ref[0].reshape(rows, D_MODEL)).astype(F32)
    out_ref[0] = (pa + gb * acc).astype(out_ref.dtype).reshape(NSEG, SEG_TILE, D_MODEL)


def _branch_b(yb, gg, part_a, w, layer):
    seg4 = lambda arr: arr.reshape(BATCH, NSEG, SEG, arr.shape[-1])
    tile = (1, NSEG, SEG_TILE, D_MODEL)
    merged = pl.pallas_call(
        _branch_b_kernel,
        grid=(BATCH, SEG // SEG_TILE),
        in_specs=[pl.BlockSpec((1, SEG_TILE, NSEG, LRU_W), lambda b, i: (b, i, 0, 0)),
                  pl.BlockSpec(tile, lambda b, i: (b, 0, i, 1)),
                  pl.BlockSpec(tile, lambda b, i: (b, 0, i, 0)),
                  pl.BlockSpec((1, LRU_W, D_MODEL), lambda b, i: (layer, 0, 0))],
        out_specs=pl.BlockSpec(tile, lambda b, i: (b, 0, i, 0)),
        out_shape=jax.ShapeDtypeStruct((BATCH, NSEG, SEG, D_MODEL), BF16),
        scratch_shapes=[pltpu.VMEM((NSEG * SEG_TILE, LRU_W), BF16)],
        compiler_params=_cparams(2),
    )(yb, seg4(gg), seg4(part_a), w)
    return merged.reshape(BATCH, SEQ, D_MODEL)


def _out_kernel(m_ref, x_ref, w_ref, fg_ref, o_ref, *, final_norm):
    y = x_ref[0] + jnp.dot(m_ref[0], w_ref[0], preferred_element_type=F32)
    if final_norm:
        ms = jnp.mean(y * y, axis=-1, keepdims=True)
        y = y * lax.rsqrt(ms + NORM_EPS) * fg_ref[...]
    o_ref[0] = y


def _out_proj(merged, x, w, layer, fg, final_norm, tm):
    row = lambda b, i: (b, i, 0)
    return pl.pallas_call(
        functools.partial(_out_kernel, final_norm=final_norm),
        grid=(BATCH, SEQ // tm),
        in_specs=[pl.BlockSpec((1, tm, D_MODEL), row),
                  pl.BlockSpec((1, tm, D_MODEL), row),
                  pl.BlockSpec((1, D_MODEL, D_MODEL), lambda b, i: (layer, 0, 0)),
                  pl.BlockSpec((1, D_MODEL), lambda b, i: (0, 0))],
        out_specs=pl.BlockSpec((1, tm, D_MODEL), row),
        out_shape=jax.ShapeDtypeStruct((BATCH, SEQ, D_MODEL), F32),
        compiler_params=_cparams(2),
    )(merged, x, w, fg)


def kernel(x, norm_g, w_in, b_if, head_g, conv_w, conv_b, w_rg, b_rg, lru_lambda,
           w_branch_a, w_branch_b, w_out, final_g):
    w_t = jnp.swapaxes(w_in, 1, 2)
    bg = jnp.pad(b_if, ((0, 0), (0, 128 - N_GATE)))
    n_blk = LRU_W // LRU_BLOCK
    w_gate = jnp.transpose(w_rg, (0, 1, 3, 4, 2, 5)).reshape(
        DEPTH, 2, n_blk, LRU_BLOCK, 2 * LRU_BLOCK).astype(BF16)
    b_gate = jnp.transpose(b_rg.reshape(DEPTH, 2, 2, n_blk, LRU_BLOCK), (0, 1, 3, 2, 4)).reshape(
        DEPTH, 2, n_blk, 1, 2 * LRU_BLOCK)
    w_a16 = w_branch_a.astype(BF16)
    w_b16 = w_branch_b.astype(BF16)
    w_o16 = w_out.astype(BF16)

    head_col = lambda j: jnp.where(j < 4, j + 4, jnp.where(j < 6, j - 2, 0))
    xz_cols = 2 * LRU_W // MM_TN

    h = x
    for l in range(DEPTH):
        hn, kt, grow, gcol = _kgate(h, norm_g[l][None, :], w_t, l, b_if[l][:, None],
                                    bg[l][None, :], tm=1024)
        head = _proj(hn, w_t, l, 0, head_col, N_HEAD // MM_TN, BF16)
        gg = _proj(hn, w_t, l, _TAIL0, lambda j: xz_cols + j, 2 * D_MODEL // MM_TN, BF16)
        xz = _proj(hn, w_t, l, _TAIL0, lambda j: j, xz_cols, F32, segmented=True)

        hf, hb = _mlstm(head, kt, grow, gcol)
        yb = _lru(xz, conv_w[l], conv_b[l][None, :], w_gate[l], b_gate[l], lru_lambda[l])

        part_a = _branch_a(hf, hb, head, gg, head_g[l][None, :], w_a16, l, tm=512)
        merged = _branch_b(yb, gg, part_a, w_b16, l)
        h = _out_proj(merged, h, w_o16, l, final_g[None, :], l == DEPTH - 1, tm=512)
    return h
```
